```python
import functools
import jax
import jax.numpy as jnp
from jax import lax
import numpy as np

D_MODEL = 1024
BATCH = 1
SEQ = 16384
DEPTH = 1
DEC_BATCH = 32
DEC_SEQ = 1
PAST_LEN = 16384
PAGE_SIZE = 128

NSA_HEADS = 8
NSA_KV_HEADS = 2
NSA_GROUP = NSA_HEADS // NSA_KV_HEADS
HEAD_DIM = 64
CMP_LEN = 32
CMP_STRIDE = 16
SEL_BLOCK = 64
N_SEL = 16
WINDOW = 512
Q_BLOCK = 128
RET_HEADS = 4
RET_DK = 128
RET_DV = 256
RET_CHUNK = 128
D_FF = 4 * D_MODEL

ROPE_THETA = 10000.0
RMS_EPS = 1e-6
GN_EPS = 1e-5
NEG_INF = -1e30
FORCE_SCORE = 1e4

NSA_Q_W = NSA_HEADS * HEAD_DIM
NSA_KV_W = 2 * NSA_KV_HEADS * HEAD_DIM
RET_QK_W = RET_HEADS * RET_DK
RET_V_W = RET_HEADS * RET_DV
SPLITS = (NSA_Q_W, NSA_KV_W, NSA_KV_W, NSA_KV_W, 3 * NSA_HEADS, RET_QK_W, RET_QK_W, RET_V_W, RET_V_W, D_MODEL, D_MODEL)
IN_WIDTH = sum(SPLITS)

kernel_name = "nsa_retention_hybrid_step"


def _split_points():
    pts, acc = [], 0
    for w in SPLITS[:-1]:
        acc += w
        pts.append(acc)
    return pts


def rms_norm(x, g):
    xf = x.astype(jnp.float32)
    y = xf * lax.rsqrt(jnp.mean(xf * xf, axis=-1, keepdims=True) + RMS_EPS)
    return (y * g.astype(jnp.float32)).astype(x.dtype)


def rope(x, pos):
    d = x.shape[-1]
    inv = ROPE_THETA ** (-jnp.arange(0, d, 2, dtype=jnp.float32) / d)
    ang = pos.astype(jnp.float32)[:, None] * inv[None, :]
    cos = jnp.cos(ang)[None, :, None, :]
    sin = jnp.sin(ang)[None, :, None, :]
    xf = x.astype(jnp.float32)
    x1, x2 = xf[..., : d // 2], xf[..., d // 2:]
    return jnp.concatenate([x1 * cos - x2 * sin, x1 * sin + x2 * cos], axis=-1).astype(x.dtype)


def rope_kv(kv, pos):
    return jnp.stack([rope(kv[:, :, 0], pos), kv[:, :, 1]], axis=2)


def nsa_compress(rows, w_cmp, b_cmp):
    B, L = rows.shape[:2]
    rb = rows.reshape(B, L // CMP_STRIDE, CMP_STRIDE, 2, NSA_KV_HEADS, HEAD_DIM).astype(jnp.float32)
    first = jnp.einsum('bnjchd,chjd->bnchd', rb, w_cmp[:, :, :CMP_STRIDE].astype(jnp.float32))
    second = jnp.einsum('bnjchd,chjd->bnchd', rb, w_cmp[:, :, CMP_STRIDE:].astype(jnp.float32))
    return first[:, :-1] + second[:, 1:] + b_cmp.astype(jnp.float32)


def nsa_core(q, qpos, cmp, kc_end, gather_sel, kwv, kwpos, gates):
    B, Q = q.shape[:2]
    qg = q.astype(jnp.float32).reshape(B, Q, NSA_KV_HEADS, NSA_GROUP, HEAD_DIM) * (HEAD_DIM ** -0.5)
    mc = (kc_end[None, :] <= qpos[:, None])[None, :, None, None, :]
    s_c = jnp.einsum('bqhgd,bnhd->bqhgn', qg, cmp[:, :, 0])
    p_c = jax.nn.softmax(jnp.where(mc, s_c, NEG_INF), axis=-1) * mc
    o_c = jnp.einsum('bqhgn,bnhd->bqhgd', p_c, cmp[:, :, 1])
    nc = cmp.shape[1]
    ns = (nc + 1) * CMP_STRIDE // SEL_BLOCK
    r = SEL_BLOCK // CMP_STRIDE
    imp = jnp.pad(p_c.sum(axis=3), ((0, 0), (0, 0), (0, 0), (1, r * ns - nc)))
    p_slc = imp[..., : r * ns].reshape(B, Q, NSA_KV_HEADS, ns, r).sum(-1) + imp[..., r::r]
    blk = jnp.arange(ns)
    qblk = (qpos // SEL_BLOCK)[:, None]
    valid = blk[None, :] * SEL_BLOCK <= qpos[:, None]
    forced = (blk[None, :] == 0) | (blk[None, :] == qblk) | (blk[None, :] == qblk - 1)
    score = jnp.where(valid[None, :, None, :], p_slc + jnp.where(forced, FORCE_SCORE, 0.0)[None, :, None, :], NEG_INF)
    _, idx = lax.top_k(score, min(N_SEL, ns))
    kk = idx.shape[-1]
    k_sel, v_sel = gather_sel(idx)
    tpos = idx[..., None] * SEL_BLOCK + jnp.arange(SEL_BLOCK)
    ms = (tpos <= qpos[None, :, None, None, None])[:, :, :, None]
    s_s = jnp.einsum('bqhgd,bqhksd->bqhgks', qg, k_sel)
    s_s = jnp.where(ms, s_s, NEG_INF).reshape(B, Q, NSA_KV_HEADS, NSA_GROUP, kk * SEL_BLOCK)
    p_s = jax.nn.softmax(s_s, axis=-1)
    o_s = jnp.einsum('bqhgm,bqhmd->bqhgd', p_s, v_sel.reshape(B, Q, NSA_KV_HEADS, kk * SEL_BLOCK, HEAD_DIM))
    kp = kwpos[None, :]
    mw = ((kp <= qpos[:, None]) & (kp > qpos[:, None] - WINDOW) & (kp >= 0))[None, :, None, None, :]
    s_w = jnp.einsum('bqhgd,bwhd->bqhgw', qg, kwv[:, :, 0])
    p_w = jax.nn.softmax(jnp.where(mw, s_w, NEG_INF), axis=-1)
    o_w = jnp.einsum('bqhgw,bwhd->bqhgd', p_w, kwv[:, :, 1])
    g = gates.reshape(B, Q, NSA_KV_HEADS, NSA_GROUP, 3)
    o = g[..., 0:1] * o_c + g[..., 1:2] * o_s + g[..., 2:3] * o_w
    return o.reshape(B, Q, NSA_Q_W)


def nsa_prompt(q, gates, kv_c, kv_s, kv_w, w_cmp, b_cmp):
    B, T = q.shape[:2]
    cmp = nsa_compress(kv_c, w_cmp, b_cmp)
    kc_end = jnp.arange(cmp.shape[1]) * CMP_STRIDE + (CMP_LEN - 1)
    sel_blocks = kv_s.reshape(B, T // SEL_BLOCK, SEL_BLOCK, 2, NSA_KV_HEADS, HEAD_DIM)
    bi = jnp.arange(B)[:, None, None, None]
    hi = jnp.arange(NSA_KV_HEADS)[None, None, :, None]

    def gather_sel(idx):
        g = sel_blocks[bi, idx, :, :, hi, :]
        return g[..., 0, :], g[..., 1, :]

    kw_pad = jnp.pad(kv_w, ((0, 0), (WINDOW, 0), (0, 0), (0, 0), (0, 0)))

    def block(s0):
        qb = lax.dynamic_slice_in_dim(q, s0, Q_BLOCK, axis=1)
        gb = lax.dynamic_slice_in_dim(gates, s0, Q_BLOCK, axis=1)
        qpos = s0 + jnp.arange(Q_BLOCK)
        kwb = lax.dynamic_slice_in_dim(kw_pad, s0, WINDOW + Q_BLOCK, axis=1)
        kwpos = s0 - WINDOW + jnp.arange(WINDOW + Q_BLOCK)
        return nsa_core(qb, qpos, cmp, kc_end, gather_sel, kwb, kwpos, gb)

    out = lax.map(block, jnp.arange(T // Q_BLOCK) * Q_BLOCK)
    new_win = kv_w[:, T - min(WINDOW, T):]
    return out.swapaxes(0, 1).reshape(B, T, NSA_Q_W), new_win


def nsa_sample(q, gates, kv_c, kv_s, kv_w, cache_cmp, cache_sel, cache_win, page_table, w_cmp, b_cmp):
    B, S = q.shape[:2]
    P = page_table.shape[1] * PAGE_SIZE
    s_pad = -(-S // SEL_BLOCK) * SEL_BLOCK
    pad = ((0, 0), (0, s_pad - S), (0, 0), (0, 0), (0, 0))
    past_c = cache_cmp[page_table].reshape(B, P, 2, NSA_KV_HEADS, HEAD_DIM)
    cmp = nsa_compress(jnp.concatenate([past_c, jnp.pad(kv_c, pad).astype(past_c.dtype)], axis=1), w_cmp, b_cmp)
    kc_end = jnp.arange(cmp.shape[1]) * CMP_STRIDE + (CMP_LEN - 1)
    n_past_blk = P // SEL_BLOCK
    bpp = PAGE_SIZE // SEL_BLOCK
    pool = cache_sel.reshape(cache_sel.shape[0], bpp, SEL_BLOCK, 2, NSA_KV_HEADS, HEAD_DIM)
    new_blocks = jnp.pad(kv_s, pad).astype(cache_sel.dtype).reshape(B, s_pad // SEL_BLOCK, SEL_BLOCK, 2, NSA_KV_HEADS, HEAD_DIM)
    bi = jnp.arange(B)[:, None, None, None]
    hi = jnp.arange(NSA_KV_HEADS)[None, None, :, None]

    def gather_sel(idx):
        jp = jnp.clip(idx, 0, n_past_blk - 1)
        phys = page_table[bi, jp // bpp]
        gp = pool[phys, jp % bpp, :, :, hi, :]
        jn = jnp.clip(idx - n_past_blk, 0, new_blocks.shape[1] - 1)
        gn = new_blocks[bi, jn, :, :, hi, :]
        g = jnp.where((idx < n_past_blk)[..., None, None, None], gp, gn)
        return g[..., 0, :], g[..., 1, :]

    wb = cache_win.shape[1]
    kwv = jnp.concatenate([cache_win, kv_w.astype(cache_win.dtype)], axis=1)
    kwpos = P - wb + jnp.arange(wb + S)
    qpos = P + jnp.arange(S)
    o = nsa_core(q, qpos, cmp, kc_end, gather_sel, kwv, kwpos, gates)
    return o, kwv[:, S:]


def ret_log_decay():
    return jnp.log1p(-jnp.exp2(-5.0 - jnp.arange(RET_HEADS, dtype=jnp.float32)))


def retention_chunk(S, q, k, v):
    lg = ret_log_decay()
    C = q.shape[1]
    i = jnp.arange(C, dtype=jnp.float32)
    diff = i[:, None] - i[None, :]
    D = jnp.where(diff >= 0, jnp.exp(jnp.maximum(diff, 0.0)[None] * lg[:, None, None]), 0.0)
    qf, kf, vf = q.astype(jnp.float32), k.astype(jnp.float32), v.astype(jnp.float32)
    att = jnp.einsum('bihd,bjhd->bhij', qf, kf) * D
    o = jnp.einsum('bhij,bjhe->bihe', att, vf)
    q_dec = jnp.exp((i[:, None] + 1.0) * lg[None, :])
    o = o + jnp.einsum('bihd,bhde->bihe', qf * q_dec[None, :, :, None], S)
    k_dec = jnp.exp((C - 1.0 - i)[:, None] * lg[None, :])
    S_new = jnp.exp(C * lg)[None, :, None, None] * S + jnp.einsum('bjhd,bjhe->bhde', kf * k_dec[None, :, :, None], vf)
    return S_new, o


def retention_prompt(q, k, v):
    B, T = q.shape[:2]
    n = T // RET_CHUNK

    def chunks(a):
        return a.reshape(B, n, RET_CHUNK, a.shape[2], a.shape[3]).swapaxes(0, 1)

    S0 = jnp.zeros((B, RET_HEADS, RET_DK, RET_DV), jnp.float32)

    def step(S, xs):
        return retention_chunk(S, *xs)

    S, o = lax.scan(step, S0, (chunks(q), chunks(k), chunks(v)))
    return o.swapaxes(0, 1).reshape(B, T, RET_HEADS, RET_DV), S


def retention_sample(q, k, v, state):
    S_new, o = retention_chunk(state.astype(jnp.float32), q, k, v)
    return o, S_new


def retention_out(o, gate, gn_w, gn_b):
    B, T = o.shape[:2]
    mu = jnp.mean(o, axis=-1, keepdims=True)
    var = jnp.mean(jnp.square(o - mu), axis=-1, keepdims=True)
    y = ((o - mu) * lax.rsqrt(var + GN_EPS)).reshape(B, T, RET_V_W)
    y = y * gn_w.astype(jnp.float32) + gn_b.astype(jnp.float32)
    return jax.nn.silu(gate.astype(jnp.float32)) * y


def decoder_layer(x, c, pos, nsa_fn, ret_fn, norm_mix, norm_mlp, w_ada, b_ada, w_in,
                  ret_gn_w, ret_gn_b, w_branch_nsa, w_branch_ret, w_out, w_up, w_down):
    B, T, _ = x.shape
    mod = (jax.nn.silu(c) @ w_ada + b_ada)[:, None, :]
    sh_a, sc_a, gt_a, sh_f, sc_f, gt_f = jnp.split(mod, 6, axis=-1)
    h = rms_norm(x, norm_mix) * (1.0 + sc_a) + sh_a
    z = h @ w_in
    q, kv_c, kv_s, kv_w, g_nsa, rq, rk, rv, rg, ga, gb = jnp.split(z, _split_points(), axis=-1)
    q = rope(q.reshape(B, T, NSA_HEADS, HEAD_DIM), pos)
    kv_c = rope_kv(kv_c.reshape(B, T, 2, NSA_KV_HEADS, HEAD_DIM), pos)
    kv_s = rope_kv(kv_s.reshape(B, T, 2, NSA_KV_HEADS, HEAD_DIM), pos)
    kv_w = rope_kv(kv_w.reshape(B, T, 2, NSA_KV_HEADS, HEAD_DIM), pos)
    g_nsa = jax.nn.sigmoid(g_nsa.reshape(B, T, NSA_HEADS, 3).astype(jnp.float32))
    o_nsa, new_win = nsa_fn(q, g_nsa, kv_c, kv_s, kv_w)
    rq = rope(rq.reshape(B, T, RET_HEADS, RET_DK), pos)
    rk = rope(rk.reshape(B, T, RET_HEADS, RET_DK), pos) * (RET_DK ** -0.5)
    rv = rv.reshape(B, T, RET_HEADS, RET_DV)
    o_ret, new_S = ret_fn(rq, rk, rv)
    y_ret = retention_out(o_ret, rg, ret_gn_w, ret_gn_b)
    y_a = o_nsa.astype(x.dtype) @ w_branch_nsa
    y_b = y_ret.astype(x.dtype) @ w_branch_ret
    mixed = (jax.nn.sigmoid(ga) * y_a + jax.nn.sigmoid(gb) * y_b) @ w_out
    x = x + gt_a * mixed
    h = rms_norm(x, norm_mlp) * (1.0 + sc_f) + sh_f
    x = x + gt_f * (jnp.square(jax.nn.relu(h @ w_up)) @ w_down)
    return x, kv_c, kv_s, new_win, new_S


def setup_inputs(seed: int = 0) -> dict:
    key = jax.random.key(seed)
    ks = jax.random.split(key, 24)
    f32 = jnp.float32
    n_pages = PAST_LEN // PAGE_SIZE
    n_used = DEC_BATCH * n_pages
    n_phys = n_used + n_used // 4
    win_buf = min(WINDOW, PAST_LEN)

    def nrm(k, shape, s=1.0):
        return s * jax.random.normal(k, shape, f32)

    page_table = jax.random.permutation(ks[8], n_phys)[:n_used].reshape(DEC_BATCH, n_pages).astype(jnp.int32)
    return {
        'x_prompt': nrm(ks[0], (BATCH, SEQ, D_MODEL)),
        'x_sample': nrm(ks[1], (DEC_BATCH, DEC_SEQ, D_MODEL)),
        'c_prompt': nrm(ks[2], (BATCH, D_MODEL)),
        'c_sample': nrm(ks[3], (DEC_BATCH, D_MODEL)),
        'cache_cmp_kv': nrm(ks[4], (DEPTH, n_phys, PAGE_SIZE, 2, NSA_KV_HEADS, HEAD_DIM)),
        'cache_sel_kv': nrm(ks[5], (DEPTH, n_phys, PAGE_SIZE, 2, NSA_KV_HEADS, HEAD_DIM)),
        'cache_win_kv': nrm(ks[6], (DEPTH, DEC_BATCH, win_buf, 2, NSA_KV_HEADS, HEAD_DIM)),
        'state_ret': nrm(ks[7], (DEPTH, DEC_BATCH, RET_HEADS, RET_DK, RET_DV), 0.1),
        'page_table': page_table,
        'norm_mix': 1.0 + nrm(ks[9], (DEPTH, D_MODEL), 0.02),
        'norm_mlp': 1.0 + nrm(ks[10], (DEPTH, D_MODEL), 0.02),
        'norm_final': 1.0 + nrm(ks[11], (D_MODEL,), 0.02),
        'w_ada': nrm(ks[12], (DEPTH, D_MODEL, 6 * D_MODEL), 0.2 * D_MODEL ** -0.5),
        'b_ada': nrm(ks[13], (DEPTH, 6 * D_MODEL), 0.02),
        'w_in': nrm(ks[14], (DEPTH, D_MODEL, IN_WIDTH), D_MODEL ** -0.5),
        'w_cmp': nrm(ks[15], (DEPTH, 2, NSA_KV_HEADS, CMP_LEN, HEAD_DIM), CMP_LEN ** -0.5),
        'b_cmp': nrm(ks[16], (DEPTH, 2, NSA_KV_HEADS, HEAD_DIM), 0.02),
        'ret_gn_w': 1.0 + nrm(ks[17], (DEPTH, RET_V_W), 0.02),
        'ret_gn_b': nrm(ks[18], (DEPTH, RET_V_W), 0.02),
        'w_branch_nsa': nrm(ks[19], (DEPTH, NSA_Q_W, D_MODEL), NSA_Q_W ** -0.5),
        'w_branch_ret': nrm(ks[20], (DEPTH, RET_V_W, D_MODEL), RET_V_W ** -0.5),
        'w_out': nrm(ks[21], (DEPTH, D_MODEL, D_MODEL), D_MODEL ** -0.5),
        'w_up': nrm(ks[22], (DEPTH, D_MODEL, D_FF), D_MODEL ** -0.5),
        'w_down': nrm(ks[23], (DEPTH, D_FF, D_MODEL), D_FF ** -0.5),
    }


def reference(x_prompt, x_sample, c_prompt, c_sample, cache_cmp_kv, cache_sel_kv, cache_win_kv, state_ret,
              page_table, norm_mix, norm_mlp, norm_final, w_ada, b_ada, w_in, w_cmp, b_cmp, ret_gn_w, ret_gn_b,
              w_branch_nsa, w_branch_ret, w_out, w_up, w_down):
    T = x_prompt.shape[1]
    S = x_sample.shape[1]
    P = page_table.shape[1] * PAGE_SIZE
    pos_p = jnp.arange(T)
    pos_s = P + jnp.arange(S)
    xp, xs = x_prompt, x_sample
    cmp_p, sel_p, win_p, ret_p = [], [], [], []
    cmp_s, sel_s, win_s, ret_s = [], [], [], []
    for l in range(DEPTH):
        lw = (norm_mix[l], norm_mlp[l], w_ada[l], b_ada[l], w_in[l], ret_gn_w[l], ret_gn_b[l],
              w_branch_nsa[l], w_branch_ret[l], w_out[l], w_up[l], w_down[l])
        nsa_p = functools.partial(nsa_prompt, w_cmp=w_cmp[l], b_cmp=b_cmp[l])
        xp, a, b, c, d = decoder_layer(xp, c_prompt, pos_p, nsa_p, retention_prompt, *lw)
        cmp_p.append(a); sel_p.append(b); win_p.append(c); ret_p.append(d)
        nsa_s = functools.partial(nsa_sample, cache_cmp=cache_cmp_kv[l], cache_sel=cache_sel_kv[l],
                                  cache_win=cache_win_kv[l], page_table=page_table, w_cmp=w_cmp[l], b_cmp=b_cmp[l])
        ret_fn = functools.partial(retention_sample, state=state_ret[l])
        xs, a, b, c, d = decoder_layer(xs, c_sample, pos_s, nsa_s, ret_fn, *lw)
        cmp_s.append(a); sel_s.append(b); win_s.append(c); ret_s.append(d)
    y_prompt = rms_norm(xp, norm_final)
    y_sample = rms_norm(xs, norm_final)
    return (y_prompt, y_sample,
            jnp.stack(cmp_p), jnp.stack(sel_p), jnp.stack(win_p), jnp.stack(ret_p),
            jnp.stack(cmp_s), jnp.stack(sel_s), jnp.stack(win_s), jnp.stack(ret_s))
```

```python
import functools

import jax
import jax.numpy as jnp
from jax import lax
from jax.experimental import pallas as pl
from jax.experimental.pallas import tpu as pltpu

D_MODEL = 1024
PAGE_SIZE = 128
NSA_HEADS = 8
NSA_KV_HEADS = 2
NSA_GROUP = NSA_HEADS // NSA_KV_HEADS
HEAD_DIM = 64
CMP_LEN = 32
CMP_STRIDE = 16
SEL_BLOCK = 64
N_SEL = 16
WINDOW = 512
RET_HEADS = 4
RET_DK = 128
RET_DV = 256
RET_CHUNK = 128
D_FF = 4 * D_MODEL
ROPE_THETA = 10000.0
RMS_EPS = 1e-6
GN_EPS = 1e-5
NEG_INF = -1e30
FORCE_SCORE = 1e4

NSA_Q_W = NSA_HEADS * HEAD_DIM
NSA_KV_W = 2 * NSA_KV_HEADS * HEAD_DIM
RET_QK_W = RET_HEADS * RET_DK
RET_V_W = RET_HEADS * RET_DV
N_GATES = 3 * NSA_HEADS

LANES = 128
SUBLANES = 8
VMEM_LIMIT_BYTES = 56 * 1024 * 1024

BF16 = jnp.bfloat16
F32 = jnp.float32

QPAD_W = NSA_HEADS * LANES
C_Q = 0
C_KVC = C_Q + QPAD_W
C_KVS = C_KVC + NSA_KV_W
C_KVW = C_KVS + NSA_KV_W
C_RQ = C_KVW + NSA_KV_W
C_RK = C_RQ + RET_QK_W
C_RV = C_RK + RET_QK_W
C_RG = C_RV + RET_V_W
C_GA = C_RG + RET_V_W
C_GB = C_GA + D_MODEL
C_GT = C_GB + D_MODEL
IN_PAD_W = C_GT + LANES


def _cparams(sem):
    return pltpu.CompilerParams(dimension_semantics=sem, vmem_limit_bytes=VMEM_LIMIT_BYTES)


def _vmem_full():
    return pl.BlockSpec(memory_space=pltpu.VMEM)


def _ada_kernel(c_ref, w_ref, b_ref, o_ref):
    c = c_ref[...]
    s = (c * jax.nn.sigmoid(c)).astype(BF16)
    o_ref[...] = jnp.dot(s, w_ref[...], preferred_element_type=F32) + b_ref[...]


def _ada_call(c, w_bf, b):
    rows = c.shape[0]
    n = w_bf.shape[1]
    tn = 1536
    return pl.pallas_call(
        _ada_kernel,
        grid=(n // tn,),
        in_specs=[pl.BlockSpec((rows, D_MODEL), lambda j: (0, 0)),
                  pl.BlockSpec((D_MODEL, tn), lambda j: (0, j)),
                  pl.BlockSpec((1, tn), lambda j: (0, j))],
        out_specs=pl.BlockSpec((rows, tn), lambda j: (0, j)),
        out_shape=jax.ShapeDtypeStruct((rows, n), F32),
        compiler_params=_cparams(("arbitrary",)),
        name="ada_mod",
    )(c, w_bf, b)


def _rope_slab(x, cos, sin_signed, half):
    lane = lax.broadcasted_iota(jnp.int32, x.shape, 1)
    if 2 * half == LANES:
        partner = pltpu.roll(x, half, 1)
    else:
        first = (lane % (2 * half)) < half
        partner = jnp.where(first, pltpu.roll(x, LANES - half, 1), pltpu.roll(x, half, 1))
    return x * cos + partner * sin_signed


def _inproj_kernel(x_ref, g_ref, sc_ref, sh_ref, w_ref, cs64_ref, cs128_ref, *out_refs, transposed):
    if transposed:
        (q_ref, kvc_ref, kvs_ref, kvw_ref, ks_ref, vsT_ref, kw_ref, vwT_ref,
         rq_ref, rk_ref, rv_ref, sg_ref, ga_ref, gb_ref, gt_ref) = out_refs
    else:
        (q_ref, kvc_ref, kvs_ref, kvw_ref,
         rq_ref, rk_ref, rv_ref, sg_ref, ga_ref, gb_ref, gt_ref) = out_refs
    x = x_ref[...]
    ms = jnp.mean(x * x, axis=-1, keepdims=True)
    h = (x * lax.rsqrt(ms + RMS_EPS) * g_ref[...]) * (1.0 + sc_ref[...]) + sh_ref[...]
    hb = h.astype(BF16)
    c64 = cs64_ref[:, :LANES]
    s64 = cs64_ref[:, LANES:]
    c128 = cs128_ref[:, :LANES]
    s128 = cs128_ref[:, LANES:]

    def proj(c0, width):
        return jnp.dot(hb, w_ref[:, c0:c0 + width], preferred_element_type=F32)

    zq = proj(C_Q, QPAD_W)
    for n in range(NSA_HEADS):
        slab = _rope_slab(zq[:, n * LANES:(n + 1) * LANES], c64, s64, HEAD_DIM // 2)
        q_ref[:, n * LANES:(n + 1) * LANES] = (slab * (HEAD_DIM ** -0.5)).astype(BF16)

    kv_refs = (kvc_ref, kvs_ref, kvw_ref)
    for i, c0 in enumerate((C_KVC, C_KVS, C_KVW)):
        z = proj(c0, NSA_KV_W)
        k = _rope_slab(z[:, :LANES], c64, s64, HEAD_DIM // 2)
        v = z[:, LANES:]
        kv_refs[i][:, :LANES] = k
        kv_refs[i][:, LANES:] = v
        if transposed and i > 0:
            k_ref, vT_ref = ((ks_ref, vsT_ref), (kw_ref, vwT_ref))[i - 1]
            k_ref[...] = k.astype(BF16)
            vT = v.T.astype(BF16)
            for t in range(vT_ref.shape[0]):
                vT_ref[t] = vT[:, t * LANES:(t + 1) * LANES]

    zr = proj(C_RQ, RET_QK_W)
    for n in range(RET_HEADS):
        rq_ref[:, n * LANES:(n + 1) * LANES] = _rope_slab(
            zr[:, n * LANES:(n + 1) * LANES], c128, s128, RET_DK // 2).astype(BF16)
    zr = proj(C_RK, RET_QK_W)
    for n in range(RET_HEADS):
        rk_ref[:, n * LANES:(n + 1) * LANES] = (_rope_slab(
            zr[:, n * LANES:(n + 1) * LANES], c128, s128, RET_DK // 2) * (RET_DK ** -0.5)).astype(BF16)
    rv_ref[...] = proj(C_RV, RET_V_W).astype(BF16)
    z = proj(C_RG, RET_V_W)
    sg_ref[...] = (z * jax.nn.sigmoid(z)).astype(BF16)
    ga_ref[...] = jax.nn.sigmoid(proj(C_GA, D_MODEL)).astype(BF16)
    gb_ref[...] = jax.nn.sigmoid(proj(C_GB, D_MODEL)).astype(BF16)
    gt = jax.nn.sigmoid(proj(C_GT, LANES))
    if transposed:
        gt_ref[...] = gt.T
    else:
        gt_ref[...] = gt


def _inproj_call(x, g, sc, sh, w_pad, cs64, cs128, *, tm, transposed):
    m = x.shape[0]
    per_row = sc.shape[0] != 1
    row = lambda i: (i, 0)
    mod_spec = pl.BlockSpec((tm, D_MODEL), row) if per_row else pl.BlockSpec((1, D_MODEL), lambda i: (0, 0))
    nt = tm // LANES
    shp = lambda w, dt: jax.ShapeDtypeStruct((m, w), dt)
    bs = lambda w: pl.BlockSpec((tm, w), row)
    out_shape = [shp(QPAD_W, BF16), shp(NSA_KV_W, F32), shp(NSA_KV_W, F32), shp(NSA_KV_W, F32)]
    out_specs = [bs(QPAD_W), bs(NSA_KV_W), bs(NSA_KV_W), bs(NSA_KV_W)]
    if transposed:
        for _ in range(2):
            out_shape += [shp(LANES, BF16), jax.ShapeDtypeStruct((m // LANES, LANES, LANES), BF16)]
            out_specs += [bs(LANES), pl.BlockSpec((nt, LANES, LANES), lambda i: (i, 0, 0))]
    out_shape += [shp(RET_QK_W, BF16), shp(RET_QK_W, BF16), shp(RET_V_W, BF16), shp(RET_V_W, BF16),
                  shp(D_MODEL, BF16), shp(D_MODEL, BF16)]
    out_specs += [bs(RET_QK_W), bs(RET_QK_W), bs(RET_V_W), bs(RET_V_W), bs(D_MODEL), bs(D_MODEL)]
    if transposed:
        out_shape.append(jax.ShapeDtypeStruct((LANES, m), F32))
        out_specs.append(pl.BlockSpec((LANES, tm), lambda i: (0, i)))
    else:
        out_shape.append(shp(LANES, F32))
        out_specs.append(bs(LANES))
    return pl.pallas_call(
        functools.partial(_inproj_kernel, transposed=transposed),
        grid=(m // tm,),
        in_specs=[pl.BlockSpec((tm, D_MODEL), row),
                  pl.BlockSpec((1, D_MODEL), lambda i: (0, 0)),
                  mod_spec, mod_spec,
                  _vmem_full(),
                  pl.BlockSpec((tm, 2 * LANES), row),
                  pl.BlockSpec((tm, 2 * LANES), row)],
        out_specs=out_specs,
        out_shape=out_shape,
        compiler_params=_cparams(("arbitrary",)),
        name="inproj_t" if transposed else "inproj_n",
    )(x, g, sc, sh, w_pad, cs64, cs128)


def _rope_tables(pos, d):
    half = d // 2
    inv = ROPE_THETA ** (-jnp.arange(0, d, 2, dtype=F32) / d)
    ang = pos.astype(F32)[:, None] * inv[None, :]
    cos = jnp.cos(ang)
    sin = jnp.sin(ang)
    reps = LANES // d
    cos_t = jnp.tile(jnp.concatenate([cos, cos], axis=1), (1, reps))
    sin_t = jnp.tile(jnp.concatenate([-sin, sin], axis=1), (1, reps))
    return jnp.concatenate([cos_t, sin_t], axis=1)


def _pad_in_weights(w_in):
    w = w_in.astype(BF16)
    k = w.shape[0]
    zeros = jnp.zeros((k, HEAD_DIM), BF16)
    cols = []
    for n in range(NSA_HEADS):
        wq = w[:, n * HEAD_DIM:(n + 1) * HEAD_DIM]
        cols += [wq, zeros] if n // NSA_GROUP == 0 else [zeros, wq]
    o = NSA_Q_W
    cols.append(w[:, o:o + 3 * NSA_KV_W])
    o += 3 * NSA_KV_W
    gates = w[:, o:o + N_GATES]
    o += N_GATES
    cols.append(w[:, o:])
    cols.append(gates)
    cols.append(jnp.zeros((k, LANES - N_GATES), BF16))
    return jnp.concatenate(cols, axis=1)


def _compress_kernel(x_ref, xn_ref, w1_ref, w2_ref, b_ref, o_ref, *, n_valid):
    g = x_ref.shape[0]
    x = x_ref[...]
    first = jnp.sum(x * w1_ref[...][None], axis=1)
    second = jnp.sum(x * w2_ref[...][None], axis=1)
    nxt = jnp.sum(xn_ref[0] * w2_ref[...], axis=0, keepdims=True)
    row = lax.broadcasted_iota(jnp.int32, first.shape, 0)
    shifted = jnp.where(row == g - 1, nxt, pltpu.roll(second, g - 1, 0))
    cmp = first + shifted + b_ref[...]
    n = pl.program_id(0) * g + row
    o_ref[...] = jnp.where(n < n_valid, cmp, 0.0)


def _cmp_weights(w_cmp, b_cmp):
    w = jnp.transpose(w_cmp.astype(F32), (2, 0, 1, 3)).reshape(CMP_LEN, NSA_KV_W)
    return w[:CMP_STRIDE], w[CMP_STRIDE:], b_cmp.astype(F32).reshape(1, NSA_KV_W)


def _compress_call(kvc, w1, w2, b):
    t = kvc.shape[0]
    ng = t // CMP_STRIDE
    g = min(ng, 128)
    x3 = kvc.reshape(ng, CMP_STRIDE, NSA_KV_W)
    return pl.pallas_call(
        functools.partial(_compress_kernel, n_valid=ng - 1),
        grid=(ng // g,),
        in_specs=[pl.BlockSpec((g, CMP_STRIDE, NSA_KV_W), lambda i: (i, 0, 0)),
                  pl.BlockSpec((1, CMP_STRIDE, NSA_KV_W), lambda i: (jnp.minimum((i + 1) * g, ng - 1), 0, 0)),
                  pl.BlockSpec((CMP_STRIDE, NSA_KV_W), lambda i: (0, 0)),
                  pl.BlockSpec((CMP_STRIDE, NSA_KV_W), lambda i: (0, 0)),
                  pl.BlockSpec((1, NSA_KV_W), lambda i: (0, 0))],
        out_specs=pl.BlockSpec((g, NSA_KV_W), lambda i: (i, 0)),
        out_shape=jax.ShapeDtypeStruct((ng, NSA_KV_W), F32),
        compiler_params=_cparams(("arbitrary",)),
        name="compress",
    )(x3, x3, w1, w2, b)


def _slab_major(a, nsel):
    r = SEL_BLOCK // CMP_STRIDE
    return a.reshape(nsel, r, a.shape[-1]).transpose(1, 0, 2).reshape(nsel * r, a.shape[-1])


def _nt_dot(a, b):
    return lax.dot_general(a, b, (((1,), (1,)), ((), ())), preferred_element_type=F32)


def _pairs_to_natural(o_ref, accs, h):
    for c in range(NSA_GROUP // 2):
        both = jnp.concatenate([accs[2 * c], accs[2 * c + 1]], axis=0)
        col = (NSA_GROUP // 2) * h + c
        o_ref[:, col * LANES:(col + 1) * LANES] = both.T


def _cmp_attn_kernel(q_ref, ck_ref, cvT_ref, blk_ref, gt_ref, oc_ref, sel_ref, *, tq, nsel, fixed_pos):
    ncp = 4 * nsel
    if fixed_pos is None:
        qpos = pl.program_id(0) * tq + lax.broadcasted_iota(jnp.int32, (1, tq), 1)
    else:
        qpos = jnp.full((1, tq), fixed_pos, jnp.int32)
    valid_c = (CMP_STRIDE * blk_ref[...] + (CMP_LEN - 1)) <= qpos
    jj = lax.broadcasted_iota(jnp.int32, (nsel, tq), 0)
    qblk = qpos // SEL_BLOCK
    valid_s = jj * SEL_BLOCK <= qpos
    forced = (jj == 0) | (jj == qblk) | (jj == qblk - 1)
    ck = ck_ref[0]
    for h in range(NSA_KV_HEADS):
        imp = jnp.zeros((ncp, tq), F32)
        accs = []
        cv = cvT_ref[0, h * HEAD_DIM:(h + 1) * HEAD_DIM, :]
        for g in range(NSA_GROUP):
            n = NSA_GROUP * h + g
            s = _nt_dot(ck, q_ref[:, n * LANES:(n + 1) * LANES])
            s = jnp.where(valid_c, s, NEG_INF)
            e = jnp.exp(s - jnp.max(s, axis=0, keepdims=True))
            p = jnp.where(valid_c, e * (1.0 / jnp.sum(e, axis=0, keepdims=True)), 0.0)
            imp = imp + p
            o = jnp.dot(cv, p.astype(BF16), preferred_element_type=F32)
            accs.append(o * gt_ref[3 * n:3 * n + 1, :])
        _pairs_to_natural(oc_ref, accs, h)
        p3 = imp[3 * nsel:]
        prev = jnp.where(jj == 0, 0.0, pltpu.roll(p3, 1, 0))
        pslc = imp[:nsel] + imp[nsel:2 * nsel] + imp[2 * nsel:3 * nsel] + p3 + prev
        work = jnp.where(valid_s, pslc + jnp.where(forced, FORCE_SCORE, 0.0), NEG_INF)
        sel = jnp.zeros((nsel, tq), jnp.bool_)
        for _ in range(min(N_SEL, nsel)):
            cand = work == jnp.max(work, axis=0, keepdims=True)
            idx = jnp.min(jnp.where(cand, jj, nsel), axis=0, keepdims=True)
            hit = jj == idx
            sel = sel | hit
            work = jnp.where(hit, -jnp.inf, work)
        sel_ref[h] = jnp.where(sel, 0.0, NEG_INF)


def _cmp_attn_call(qpad, ck, cvT, gT, *, tq, fixed_pos=None):
    t = qpad.shape[0]
    nsel = ck.shape[1] // (SEL_BLOCK // CMP_STRIDE)
    per_step = ck.shape[0] != 1
    kv_idx = (lambda i: (i, 0, 0)) if per_step else (lambda i: (0, 0, 0))
    r = jnp.arange(4 * nsel, dtype=jnp.int32)
    blk = jnp.broadcast_to(((SEL_BLOCK // CMP_STRIDE) * (r % nsel) + r // nsel)[:, None], (4 * nsel, tq))
    return pl.pallas_call(
        functools.partial(_cmp_attn_kernel, tq=tq, nsel=nsel, fixed_pos=fixed_pos),
        grid=(t // tq,),
        in_specs=[pl.BlockSpec((tq, QPAD_W), lambda i: (i, 0)),
                  pl.BlockSpec((1, 4 * nsel, LANES), kv_idx),
                  pl.BlockSpec((1, LANES, 4 * nsel), kv_idx),
                  pl.BlockSpec((4 * nsel, tq), lambda i: (0, 0)),
                  pl.BlockSpec((LANES, tq), lambda i: (0, i))],
        out_specs=[pl.BlockSpec((tq, NSA_Q_W), lambda i: (i, 0)),
                   pl.BlockSpec((NSA_KV_HEADS, nsel, tq), lambda i: (0, 0, i))],
        out_shape=[jax.ShapeDtypeStruct((t, NSA_Q_W), F32),
                   jax.ShapeDtypeStruct((NSA_KV_HEADS, nsel, t), F32)],
        compiler_params=_cparams(("arbitrary",)),
        name="cmp_attn_topk_n" if per_step else "cmp_attn_topk_t",
    )(qpad, ck, cvT, blk, gT)


def _stack_q(q_ref, h):
    return jnp.concatenate([q_ref[:, (NSA_GROUP * h + g) * LANES:(NSA_GROUP * h + g + 1) * LANES]
                            for g in range(NSA_GROUP)], axis=0)


def _gate_row(gt_ref, h, branch):
    return jnp.concatenate([gt_ref[3 * (NSA_GROUP * h + g) + branch:3 * (NSA_GROUP * h + g) + branch + 1, :]
                            for g in range(NSA_GROUP)], axis=1)


def _vT_tile(vT_ref, first, count, h):
    v = vT_ref[pl.ds(first, count)]
    return jnp.concatenate([v[t, h * HEAD_DIM:(h + 1) * HEAD_DIM, :] for t in range(count)], axis=1)


def _sel_attn_kernel(q_ref, k_ref, vT_ref, sel_ref, gt_ref, o_ref, *, tq, tk):
    s0 = pl.program_id(0) * tq
    kd = s0 // tk
    nb = tk // SEL_BLOCK
    r = NSA_GROUP * tq
    qpos = s0 + lax.broadcasted_iota(jnp.int32, (1, tq), 1)
    qpos = jnp.concatenate([qpos] * NSA_GROUP, axis=1)
    krow = lax.broadcasted_iota(jnp.int32, (tk, r), 0)
    for h in range(NSA_KV_HEADS):
        qh = _stack_q(q_ref, h)

        def step(kt, carry, causal, qh=qh, h=h):
            m, l, acc = carry
            kb = k_ref[pl.ds(pl.multiple_of(kt * tk, tk), tk), :]
            s = _nt_dot(kb, qh)
            rows = sel_ref[h, pl.ds(pl.multiple_of(kt * nb, nb), nb), :]
            bias = jnp.concatenate([jnp.broadcast_to(rows[b:b + 1], (SEL_BLOCK, tq)) for b in range(nb)], axis=0)
            s = s + jnp.concatenate([bias] * NSA_GROUP, axis=1)
            if causal:
                s = jnp.where(kt * tk + krow <= qpos, s, NEG_INF)
            m_new = jnp.maximum(m, jnp.max(s, axis=0, keepdims=True))
            p = jnp.exp(s - m_new)
            alpha = jnp.exp(m - m_new)
            l = alpha * l + jnp.sum(p, axis=0, keepdims=True)
            pv = jnp.dot(_vT_tile(vT_ref, kt * (tk // LANES), tk // LANES, h), p.astype(BF16),
                         preferred_element_type=F32)
            return m_new, l, alpha * acc + pv

        init = (jnp.full((1, r), NEG_INF, F32), jnp.zeros((1, r), F32), jnp.zeros((HEAD_DIM, r), F32))
        carry = lax.fori_loop(0, kd, functools.partial(step, causal=False), init)
        m, l, acc = step(kd, carry, True)
        o = acc * (1.0 / l) * _gate_row(gt_ref, h, 1)
        _pairs_to_natural(o_ref, [o[:, g * tq:(g + 1) * tq] for g in range(NSA_GROUP)], h)


def _sel_attn_call(qpad, ks, vsT, selb, gT, *, tq, tk):
    t = qpad.shape[0]
    nsel = t // SEL_BLOCK
    return pl.pallas_call(
        functools.partial(_sel_attn_kernel, tq=tq, tk=tk),
        grid=(t // tq,),
        in_specs=[pl.BlockSpec((tq, QPAD_W), lambda i: (i, 0)),
                  _vmem_full(), _vmem_full(),
                  pl.BlockSpec((NSA_KV_HEADS, nsel, tq), lambda i: (0, 0, i)),
                  pl.BlockSpec((LANES, tq), lambda i: (0, i))],
        out_specs=pl.BlockSpec((tq, NSA_Q_W), lambda i: (i, 0)),
        out_shape=jax.ShapeDtypeStruct((t, NSA_Q_W), F32),
        compiler_params=_cparams(("arbitrary",)),
        name="sel_attn",
    )(qpad, ks, vsT, selb, gT)


def _win_attn_kernel(q_ref, k_ref, vT_ref, gt_ref, o_ref, *, tq):
    band = WINDOW + tq
    s0 = pl.program_id(0) * tq
    start = pl.multiple_of(jnp.maximum(s0 - WINDOW, 0), LANES)
    r = NSA_GROUP * tq
    qpos = s0 + lax.broadcasted_iota(jnp.int32, (1, tq), 1)
    qpos = jnp.concatenate([qpos] * NSA_GROUP, axis=1)
    kp = start + lax.broadcasted_iota(jnp.int32, (band, r), 0)
    valid = (kp <= qpos) & (kp > qpos - WINDOW)
    kb = k_ref[pl.ds(start, band), :]
    for h in range(NSA_KV_HEADS):
        s = jnp.where(valid, _nt_dot(kb, _stack_q(q_ref, h)), NEG_INF)
        e = jnp.exp(s - jnp.max(s, axis=0, keepdims=True))
        p = e * (1.0 / jnp.sum(e, axis=0, keepdims=True))
        o = jnp.dot(_vT_tile(vT_ref, start // LANES, band // LANES, h), p.astype(BF16),
                    preferred_element_type=F32)
        o = o * _gate_row(gt_ref, h, 2)
        _pairs_to_natural(o_ref, [o[:, g * tq:(g + 1) * tq] for g in range(NSA_GROUP)], h)


def _win_attn_call(qpad, kw, vwT, gT, *, tq):
    t = qpad.shape[0]
    return pl.pallas_call(
        functools.partial(_win_attn_kernel, tq=tq),
        grid=(t // tq,),
        in_specs=[pl.BlockSpec((tq, QPAD_W), lambda i: (i, 0)),
                  _vmem_full(), _vmem_full(),
                  pl.BlockSpec((LANES, tq), lambda i: (0, i))],
        out_specs=pl.BlockSpec((tq, NSA_Q_W), lambda i: (i, 0)),
        out_shape=jax.ShapeDtypeStruct((t, NSA_Q_W), F32),
        compiler_params=_cparams(("arbitrary",)),
        name="win_attn",
    )(qpad, kw, vwT, gT)


def _ret_tables(c):
    lg = jnp.log1p(-jnp.exp2(-5.0 - jnp.arange(RET_HEADS, dtype=F32)))
    i = jnp.arange(c, dtype=F32)
    diff = i[:, None] - i[None, :]
    dmat = jnp.where(diff >= 0, jnp.exp(jnp.maximum(diff, 0.0)[None] * lg[:, None, None]), 0.0)
    qdec = jnp.exp((i[None, :] + 1.0) * lg[:, None])
    kdec = jnp.exp((c - 1.0 - i)[None, :] * lg[:, None])
    gc = jnp.exp(c * lg)
    return (dmat,
            jnp.broadcast_to(qdec[:, :, None], (RET_HEADS, c, RET_DV)),
            jnp.broadcast_to(kdec[:, :, None], (RET_HEADS, c, RET_DK)),
            jnp.broadcast_to(gc[:, None, None], (RET_HEADS, SUBLANES, RET_DV)))


def _group_norm_gate(o, sg, gnw, gnb):
    mu = jnp.mean(o, axis=-1, keepdims=True)
    d = o - mu
    var = jnp.mean(d * d, axis=-1, keepdims=True)
    return sg.astype(F32) * (d * lax.rsqrt(var + GN_EPS) * gnw + gnb)


def _ret_kernel(rq_ref, rk_ref, rv_ref, sg_ref, dm_ref, qd_ref, kd_ref, gc_ref, gnw_ref, gnb_ref,
                y_ref, st_ref, s_scr, *, tr):
    @pl.when(pl.program_id(0) == 0)
    def _():
        s_scr[...] = jnp.zeros_like(s_scr)

    c = RET_CHUNK
    for ci in range(tr // c):
        rows = slice(ci * c, (ci + 1) * c)
        for h in range(RET_HEADS):
            q = rq_ref[rows, h * RET_DK:(h + 1) * RET_DK]
            k = rk_ref[rows, h * RET_DK:(h + 1) * RET_DK]
            v = rv_ref[rows, h * RET_DV:(h + 1) * RET_DV]
            att = _nt_dot(q, k) * dm_ref[h]
            s_old = s_scr[h]
            o = jnp.dot(att.astype(BF16), v, preferred_element_type=F32)
            o = o + jnp.dot(q, s_old.astype(BF16), preferred_element_type=F32) * qd_ref[h]
            kdT = (k.astype(F32) * kd_ref[h]).T.astype(BF16)
            s_scr[h] = gc_ref[h][:1] * s_old + jnp.dot(kdT, v, preferred_element_type=F32)
            cols = slice(h * RET_DV, (h + 1) * RET_DV)
            y_ref[rows, cols] = _group_norm_gate(o, sg_ref[rows, cols], gnw_ref[:, cols], gnb_ref[:, cols]).astype(BF16)
    st_ref[...] = s_scr[...]


def _ret_call(rq, rk, rv, sg, gnw, gnb, *, tr):
    t = rq.shape[0]
    dm, qd, kd, gc = _ret_tables(RET_CHUNK)
    row = lambda i: (i, 0)
    full3 = lambda shape: pl.BlockSpec(shape, lambda i: (0, 0, 0))
    return pl.pallas_call(
        functools.partial(_ret_kernel, tr=tr),
        grid=(t // tr,),
        in_specs=[pl.BlockSpec((tr, RET_QK_W), row), pl.BlockSpec((tr, RET_QK_W), row),
                  pl.BlockSpec((tr, RET_V_W), row), pl.BlockSpec((tr, RET_V_W), row),
                  full3(dm.shape), full3(qd.shape), full3(kd.shape), full3(gc.shape),
                  pl.BlockSpec((1, RET_V_W), lambda i: (0, 0)), pl.BlockSpec((1, RET_V_W), lambda i: (0, 0))],
        out_specs=[pl.BlockSpec((tr, RET_V_W), row), full3((RET_HEADS, RET_DK, RET_DV))],
        out_shape=[jax.ShapeDtypeStruct((t, RET_V_W), BF16),
                   jax.ShapeDtypeStruct((RET_HEADS, RET_DK, RET_DV), F32)],
        scratch_shapes=[pltpu.VMEM((RET_HEADS, RET_DK, RET_DV), F32)],
        compiler_params=_cparams(("arbitrary",)),
        name="retention",
    )(rq, rk, rv, sg, dm, qd, kd, gc, gnw, gnb)


def _rms(x, g):
    return x * lax.rsqrt(jnp.mean(x * x, axis=-1, keepdims=True) + RMS_EPS) * g


def _post_kernel(x_ref, oc_ref, os_ref, ow_ref, yr_ref, ga_ref, gb_ref,
                 gta_ref, shf_ref, scf_ref, gtf_ref, gm_ref, gf_ref,
                 wa_ref, wb_ref, wo_ref, wu_ref, wd_ref, y_ref, *, ff_chunk):
    o_nsa = (oc_ref[...] + os_ref[...] + ow_ref[...]).astype(BF16)
    y_a = jnp.dot(o_nsa, wa_ref[...], preferred_element_type=F32)
    y_b = jnp.dot(yr_ref[...], wb_ref[...], preferred_element_type=F32)
    merged = ga_ref[...].astype(F32) * y_a + gb_ref[...].astype(F32) * y_b
    mixed = jnp.dot(merged.astype(BF16), wo_ref[...], preferred_element_type=F32)
    x1 = x_ref[...] + gta_ref[...] * mixed
    h2 = (_rms(x1, gm_ref[...]) * (1.0 + scf_ref[...]) + shf_ref[...]).astype(BF16)
    mlp = jnp.zeros_like(x1)
    for c0 in range(0, D_FF, ff_chunk):
        u = jnp.maximum(jnp.dot(h2, wu_ref[:, c0:c0 + ff_chunk], preferred_element_type=F32), 0.0)
        mlp = mlp + jnp.dot((u * u).astype(BF16), wd_ref[c0:c0 + ff_chunk, :], preferred_element_type=F32)
    x2 = x1 + gtf_ref[...] * mlp
    y_ref[...] = _rms(x2, gf_ref[...])


def _post_call(x, oc, osel, ow, yr, ga, gb, gta, shf, scf, gtf, gm, gf, wa, wb, wo, wu, wd, *, tm):
    m = x.shape[0]
    per_row = gta.shape[0] != 1
    row = lambda i: (i, 0)
    one = lambda i: (0, 0)
    mod_spec = pl.BlockSpec((tm, D_MODEL), row) if per_row else pl.BlockSpec((1, D_MODEL), one)
    return pl.pallas_call(
        functools.partial(_post_kernel, ff_chunk=1024),
        grid=(m // tm,),
        in_specs=[pl.BlockSpec((tm, D_MODEL), row),
                  pl.BlockSpec((tm, NSA_Q_W), row), pl.BlockSpec((tm, NSA_Q_W), row), pl.BlockSpec((tm, NSA_Q_W), row),
                  pl.BlockSpec((tm, RET_V_W), row), pl.BlockSpec((tm, D_MODEL), row), pl.BlockSpec((tm, D_MODEL), row),
                  mod_spec, mod_spec, mod_spec, mod_spec,
                  pl.BlockSpec((1, D_MODEL), one), pl.BlockSpec((1, D_MODEL), one),
                  _vmem_full(), _vmem_full(), _vmem_full(), _vmem_full(), _vmem_full()],
        out_specs=pl.BlockSpec((tm, D_MODEL), row),
        out_shape=jax.ShapeDtypeStruct((m, D_MODEL), F32),
        compiler_params=_cparams(("arbitrary",)),
        name="post_t" if not per_row else "post_n",
    )(x, oc, osel, ow, yr, ga, gb, gta, shf, scf, gtf, gm, gf, wa, wb, wo, wu, wd)


def _prompt_path(x, mod, wts, pos, w_cmp, b_cmp):
    t = x.shape[0]
    sh_a, sc_a, gt_a, sh_f, sc_f, gt_f = [mod[:, i * D_MODEL:(i + 1) * D_MODEL] for i in range(6)]
    (q, kvc, kvs, kvw, ks, vsT, kw, vwT, rq, rk, rv, sg, ga, gb, gT) = _inproj_call(
        x, wts["norm_mix"], sc_a, sh_a, wts["w_in"], _rope_tables(pos, HEAD_DIM), _rope_tables(pos, RET_DK),
        tm=512, transposed=True)
    w1, w2, b = _cmp_weights(w_cmp, b_cmp)
    cmp = _compress_call(kvc, w1, w2, b)
    nsel = t // SEL_BLOCK
    ck = _slab_major(cmp[:, :LANES], nsel).astype(BF16)
    cvT = _slab_major(cmp[:, LANES:], nsel).T.astype(BF16)
    oc, selb = _cmp_attn_call(q, ck[None], cvT[None], gT, tq=128)
    osel = _sel_attn_call(q, ks, vsT, selb, gT, tq=128, tk=512)
    ow = _win_attn_call(q, kw, vwT, gT, tq=128)
    yr, state = _ret_call(rq, rk, rv, sg, wts["gn_w"], wts["gn_b"], tr=512)
    y = _post_call(x, oc, osel, ow, yr, ga, gb, gt_a, sh_f, sc_f, gt_f, wts["norm_mlp"], wts["norm_final"],
                   wts["w_a"], wts["w_b"], wts["w_o"], wts["w_u"], wts["w_d"], tm=256)
    return y, kvc, kvs, kvw, state


PAGES_PER_STEP = 16
GROUPS_PER_PAGE = PAGE_SIZE // CMP_STRIDE


def _paged_compress_kernel(pt_ref, *refs):
    page_refs = refs[:PAGES_PER_STEP]
    nxt_ref, new_ref, w1_ref, w2_ref, b_ref, o_ref = refs[PAGES_PER_STEP:]
    x = jnp.concatenate([r[0] for r in page_refs], axis=0)
    g = x.shape[0]
    first = jnp.sum(x * w1_ref[...][None], axis=1)
    second = jnp.sum(x * w2_ref[...][None], axis=1)
    nxt_page = jnp.sum(nxt_ref[0, 0] * w2_ref[...], axis=0, keepdims=True)
    nxt_new = new_ref[0] * w2_ref[0:1, :]
    last = pl.program_id(1) == pl.num_programs(1) - 1
    nxt = jnp.where(last, nxt_new, nxt_page)
    row = lax.broadcasted_iota(jnp.int32, first.shape, 0)
    shifted = jnp.where(row == g - 1, nxt, pltpu.roll(second, g - 1, 0))
    o_ref[0] = first + shifted + b_ref[...]


def _paged_compress_call(cache, page_table, new_row, w1, w2, b):
    nb, npages = page_table.shape
    steps = npages // PAGES_PER_STEP
    c4 = cache.reshape(cache.shape[0], GROUPS_PER_PAGE, CMP_STRIDE, NSA_KV_W)
    page_spec = lambda k: pl.BlockSpec((1, GROUPS_PER_PAGE, CMP_STRIDE, NSA_KV_W),
                                       lambda bi, i, pt: (pt[bi, PAGES_PER_STEP * i + k], 0, 0, 0))
    nxt_spec = pl.BlockSpec((1, GROUPS_PER_PAGE, CMP_STRIDE, NSA_KV_W),
                            lambda bi, i, pt: (pt[bi, jnp.minimum(PAGES_PER_STEP * (i + 1), npages - 1)], 0, 0, 0))
    const = lambda shape: pl.BlockSpec(shape, lambda bi, i, pt: (0, 0))
    g = PAGES_PER_STEP * GROUPS_PER_PAGE
    return pl.pallas_call(
        _paged_compress_kernel,
        grid_spec=pltpu.PrefetchScalarGridSpec(
            num_scalar_prefetch=1,
            grid=(nb, steps),
            in_specs=[page_spec(k) for k in range(PAGES_PER_STEP)] + [
                nxt_spec,
                pl.BlockSpec((1, 1, NSA_KV_W), lambda bi, i, pt: (bi, 0, 0)),
                const((CMP_STRIDE, NSA_KV_W)), const((CMP_STRIDE, NSA_KV_W)), const((1, NSA_KV_W))],
            out_specs=pl.BlockSpec((1, g, NSA_KV_W), lambda bi, i, pt: (bi, i, 0)),
        ),
        out_shape=jax.ShapeDtypeStruct((nb, npages * GROUPS_PER_PAGE, NSA_KV_W), F32),
        compiler_params=_cparams(("arbitrary", "arbitrary")),
        name="paged_compress",
    )(page_table, *([c4] * (PAGES_PER_STEP + 1)), new_row, w1, w2, b)


def _softmax_lanes(s):
    e = jnp.exp(s - jnp.max(s, axis=-1, keepdims=True))
    return e * (1.0 / jnp.sum(e, axis=-1, keepdims=True))


def _sample_sel_kernel(idx_ref, pt_ref, q_ref, *refs, n_past_blk):
    blk_refs = refs[:N_SEL]
    new_ref, g_ref, o_ref = refs[N_SEL:]
    bi = pl.program_id(0)
    h = pl.program_id(1)
    kv = jnp.concatenate([r[0] for r in blk_refs], axis=0)
    nk = kv.shape[0]
    row = lax.broadcasted_iota(jnp.int32, (nk, LANES), 0)
    lane = lax.broadcasted_iota(jnp.int32, (SUBLANES, nk), 1)
    first_new = jnp.zeros((nk, LANES), jnp.int32)
    dead = jnp.zeros((SUBLANES, nk), jnp.int32)
    for kk in range(N_SEL):
        is_new = (idx_ref[bi, h * N_SEL + kk] >= n_past_blk).astype(jnp.int32)
        first_new = first_new + jnp.where(row == kk * SEL_BLOCK, is_new, 0)
        dead = dead + jnp.where((lane > kk * SEL_BLOCK) & (lane < (kk + 1) * SEL_BLOCK), is_new, 0)
    new = new_ref[0]
    k_all = jnp.where(first_new > 0, new[:, :LANES], kv[:, :LANES]).astype(BF16)
    v_all = jnp.where(first_new > 0, new[:, LANES:], kv[:, LANES:]).astype(BF16)
    s = _nt_dot(q_ref[0, 0], k_all)
    p = _softmax_lanes(jnp.where(dead > 0, NEG_INF, s))
    o_ref[0, 0] = jnp.dot(p.astype(BF16), v_all, preferred_element_type=F32) * g_ref[0, 0]


def _sample_sel_call(idx, page_table, q4, cache, new_row, g4):
    nb = q4.shape[0]
    bpp = PAGE_SIZE // SEL_BLOCK
    n_past_blk = page_table.shape[1] * bpp
    c3 = cache.reshape(cache.shape[0] * bpp, SEL_BLOCK, NSA_KV_W)

    def blk_map(k):
        def f(bi, h, idx_ref, pt_ref):
            j = jnp.minimum(idx_ref[bi, h * N_SEL + k], n_past_blk - 1)
            return (pt_ref[bi, j // bpp] * bpp + j % bpp, 0, 0)
        return f

    bh = lambda bi, h, idx_ref, pt_ref: (bi, h, 0, 0)
    return pl.pallas_call(
        functools.partial(_sample_sel_kernel, n_past_blk=n_past_blk),
        grid_spec=pltpu.PrefetchScalarGridSpec(
            num_scalar_prefetch=2,
            grid=(nb, NSA_KV_HEADS),
            in_specs=[pl.BlockSpec((1, 1, SUBLANES, LANES), bh)]
            + [pl.BlockSpec((1, SEL_BLOCK, NSA_KV_W), blk_map(k)) for k in range(N_SEL)]
            + [pl.BlockSpec((1, 1, NSA_KV_W), lambda bi, h, idx_ref, pt_ref: (bi, 0, 0)),
               pl.BlockSpec((1, 1, SUBLANES, LANES), bh)],
            out_specs=pl.BlockSpec((1, 1, SUBLANES, LANES), bh),
        ),
        out_shape=jax.ShapeDtypeStruct((nb, NSA_KV_HEADS, SUBLANES, LANES), F32),
        compiler_params=_cparams(("arbitrary", "arbitrary")),
        name="sample_sel_attn",
    )(idx.reshape(nb, NSA_KV_HEADS * N_SEL), page_table, q4, *([c3] * N_SEL), new_row, g4)


def _sample_win_kernel(q_ref, cw_ref, new_ref, g_ref, nw_ref, o_ref):
    cw = cw_ref[0]
    w = cw.shape[0]
    row = lax.broadcasted_iota(jnp.int32, cw.shape, 0)
    nw = jnp.where(row == w - 1, new_ref[0], pltpu.roll(cw, w - 1, 0))
    nw_ref[0] = nw
    k = nw[:, :LANES].astype(BF16)
    v = nw[:, LANES:].astype(BF16)
    for h in range(NSA_KV_HEADS):
        p = _softmax_lanes(_nt_dot(q_ref[0, h], k))
        o_ref[0, h] = jnp.dot(p.astype(BF16), v, preferred_element_type=F32) * g_ref[0, h]


def _sample_win_call(q4, cache_win, new_row, g4):
    nb, w = cache_win.shape[:2]
    b4 = lambda bi: (bi, 0, 0, 0)
    b3 = lambda bi: (bi, 0, 0)
    return pl.pallas_call(
        _sample_win_kernel,
        grid=(nb,),
        in_specs=[pl.BlockSpec((1, NSA_KV_HEADS, SUBLANES, LANES), b4),
                  pl.BlockSpec((1, w, NSA_KV_W), b3),
                  pl.BlockSpec((1, 1, NSA_KV_W), b3),
                  pl.BlockSpec((1, NSA_KV_HEADS, SUBLANES, LANES), b4)],
        out_specs=[pl.BlockSpec((1, w, NSA_KV_W), b3),
                   pl.BlockSpec((1, NSA_KV_HEADS, SUBLANES, LANES), b4)],
        out_shape=[jax.ShapeDtypeStruct((nb, w, NSA_KV_W), F32),
                   jax.ShapeDtypeStruct((nb, NSA_KV_HEADS, SUBLANES, LANES), F32)],
        compiler_params=_cparams(("arbitrary",)),
        name="sample_win_attn",
    )(q4, cache_win, new_row, g4)


def _sample_ret_kernel(q_ref, k_ref, kc_ref, v_ref, sg_ref, s_ref, dec_ref, gnw_ref, gnb_ref, y_ref, so_ref):
    for h in range(RET_HEADS):
        q = q_ref[0, h]
        k = k_ref[0, h].astype(F32)
        v = v_ref[0, h].astype(F32)
        s_old = s_ref[0, h]
        gamma = dec_ref[h]
        qk = jnp.sum(q.astype(F32) * k, axis=-1, keepdims=True)
        o = qk * v + jnp.dot(q, s_old.astype(BF16), preferred_element_type=F32) * gamma
        so_ref[0, h] = gamma[:1] * s_old + kc_ref[0, h] * v[:1]
        cols = slice(h * RET_DV, (h + 1) * RET_DV)
        y_ref[0, h] = _group_norm_gate(o, sg_ref[0, h], gnw_ref[:, cols], gnb_ref[:, cols])


def _sample_ret_call(q4, k4, kc4, v4, sg4, state, gnw, gnb):
    nb = q4.shape[0]
    _, _, _, gc = _ret_tables(1)
    b4 = lambda bi: (bi, 0, 0, 0)
    spec = lambda a: pl.BlockSpec((1,) + a.shape[1:], b4)
    return pl.pallas_call(
        _sample_ret_kernel,
        grid=(nb,),
        in_specs=[spec(q4), spec(k4), spec(kc4), spec(v4), spec(sg4), spec(state),
                  pl.BlockSpec(gc.shape, lambda bi: (0, 0, 0)),
                  pl.BlockSpec((1, RET_V_W), lambda bi: (0, 0)), pl.BlockSpec((1, RET_V_W), lambda bi: (0, 0))],
        out_specs=[pl.BlockSpec((1, RET_HEADS, SUBLANES, RET_DV), b4), spec(state)],
        out_shape=[jax.ShapeDtypeStruct((nb, RET_HEADS, SUBLANES, RET_DV), F32),
                   jax.ShapeDtypeStruct(state.shape, F32)],
        compiler_params=_cparams(("arbitrary",)),
        name="sample_retention",
    )(q4, k4, kc4, v4, sg4, state, gc, gnw, gnb)


def _row0(a, rows):
    return jnp.concatenate([a[..., None, :], jnp.zeros(a.shape[:-1] + (rows - 1, a.shape[-1]), a.dtype)], axis=-2)


def _sample_path(x, mod, wts, pos, w_cmp, b_cmp, cache_cmp, cache_sel, cache_win, state, page_table):
    nb = x.shape[0]
    p_len = page_table.shape[1] * PAGE_SIZE
    sh_a, sc_a, gt_a, sh_f, sc_f, gt_f = [mod[:, i * D_MODEL:(i + 1) * D_MODEL] for i in range(6)]
    (q, kvc, kvs, kvw, rq, rk, rv, sg, ga, gb, gates) = _inproj_call(
        x, wts["norm_mix"], sc_a, sh_a, wts["w_in"], _rope_tables(pos, HEAD_DIM), _rope_tables(pos, RET_DK),
        tm=nb, transposed=False)
    w1, w2, b = _cmp_weights(w_cmp, b_cmp)
    n_phys = cache_cmp.shape[0]
    cmp_main = _paged_compress_call(cache_cmp.reshape(n_phys, PAGE_SIZE, NSA_KV_W), page_table, kvc[:, None, :], w1, w2, b)
    nc_main = cmp_main.shape[1]
    tail = jnp.concatenate([(kvc * w1[0:1] + b)[:, None, :],
                            jnp.broadcast_to(b[None], (nb, 2, NSA_KV_W))], axis=1)
    nsel = -(-(p_len // SEL_BLOCK + 1) // 32) * 32
    ncp = nsel * (SEL_BLOCK // CMP_STRIDE)
    cmp_all = jnp.concatenate([cmp_main, tail, jnp.zeros((nb, ncp - nc_main - 3, NSA_KV_W), F32)], axis=1)
    slab = jax.vmap(lambda a: _slab_major(a, nsel))
    ck = slab(cmp_all[:, :, :LANES]).astype(BF16)
    cvT = jnp.swapaxes(slab(cmp_all[:, :, LANES:]), 1, 2).astype(BF16)
    tq = LANES
    q_lane0 = _row0(q, tq).reshape(nb * tq, QPAD_W)
    gT_lane0 = jnp.transpose(_row0(gates, tq), (2, 0, 1)).reshape(LANES, nb * tq)
    oc_rows, selb = _cmp_attn_call(q_lane0, ck, cvT, gT_lane0, tq=tq, fixed_pos=p_len)
    oc = oc_rows.reshape(nb, tq, NSA_Q_W)[:, 0]
    chosen = selb.reshape(NSA_KV_HEADS, nsel, nb, tq)[:, :, :, 0] == 0.0
    order = jnp.argsort(jnp.where(chosen, 0, 1), axis=1, stable=True)[:, :N_SEL]
    idx = jnp.transpose(order, (2, 0, 1)).astype(jnp.int32)
    q4 = q.reshape(nb, NSA_KV_HEADS, NSA_GROUP, LANES)
    q4 = jnp.concatenate([q4, jnp.zeros_like(q4)], axis=2)
    g3 = gates[:, :N_GATES].reshape(nb, NSA_KV_HEADS, NSA_GROUP, 3)
    g4 = lambda br: jnp.broadcast_to(
        jnp.concatenate([g3[..., br], jnp.zeros_like(g3[..., br])], axis=2)[..., None], (nb, NSA_KV_HEADS, SUBLANES, LANES))
    os_raw = _sample_sel_call(idx, page_table, q4, cache_sel.reshape(n_phys, PAGE_SIZE, NSA_KV_W), kvs[:, None, :], g4(1))
    new_win, ow_raw = _sample_win_call(q4, cache_win.reshape(nb, -1, NSA_KV_W), kvw[:, None, :], g4(2))

    def natural(raw):
        parts = [raw[:, h, :NSA_GROUP, h * HEAD_DIM:(h + 1) * HEAD_DIM] for h in range(NSA_KV_HEADS)]
        return jnp.concatenate(parts, axis=1).reshape(nb, NSA_Q_W)

    rq4 = _row0(rq.reshape(nb, RET_HEADS, RET_DK), SUBLANES)
    rk3 = rk.reshape(nb, RET_HEADS, RET_DK)
    y_raw, new_state = _sample_ret_call(
        rq4, _row0(rk3, SUBLANES), rk3.astype(F32)[..., None], _row0(rv.reshape(nb, RET_HEADS, RET_DV), SUBLANES),
        _row0(sg.reshape(nb, RET_HEADS, RET_DV), SUBLANES), state, wts["gn_w"], wts["gn_b"])
    yr = y_raw[:, :, 0, :].reshape(nb, RET_V_W).astype(BF16)
    y = _post_call(x, oc, natural(os_raw), natural(ow_raw), yr, ga, gb, gt_a, sh_f, sc_f, gt_f,
                   wts["norm_mlp"], wts["norm_final"], wts["w_a"], wts["w_b"], wts["w_o"], wts["w_u"], wts["w_d"], tm=nb)
    return y, kvc, kvs, new_win, new_state


def kernel(x_prompt, x_sample, c_prompt, c_sample, cache_cmp_kv, cache_sel_kv, cache_win_kv, state_ret, page_table,
           norm_mix, norm_mlp, norm_final, w_ada, b_ada, w_in, w_cmp, b_cmp, ret_gn_w, ret_gn_b,
           w_branch_nsa, w_branch_ret, w_out, w_up, w_down):
    assert x_prompt.shape[0] == 1 and x_sample.shape[1] == 1 and norm_mix.shape[0] == 1
    t = x_prompt.shape[1]
    nb = x_sample.shape[0]
    p_len = page_table.shape[1] * PAGE_SIZE
    wts = dict(norm_mix=norm_mix[0][None], norm_mlp=norm_mlp[0][None], norm_final=norm_final[None],
               w_in=_pad_in_weights(w_in[0]), gn_w=ret_gn_w[0][None], gn_b=ret_gn_b[0][None],
               w_a=w_branch_nsa[0].astype(BF16), w_b=w_branch_ret[0].astype(BF16), w_o=w_out[0].astype(BF16),
               w_u=w_up[0].astype(BF16), w_d=w_down[0].astype(BF16))
    c_all = jnp.concatenate([c_prompt, c_sample], axis=0)
    c_all = jnp.pad(c_all, ((0, (-c_all.shape[0]) % SUBLANES), (0, 0)))
    mod = _ada_call(c_all, w_ada[0].astype(BF16), b_ada[0][None])
    yp, kvc_p, kvs_p, kvw_p, st_p = _prompt_path(x_prompt[0], mod[:1], wts, jnp.arange(t), w_cmp[0], b_cmp[0])
    ys, kvc_s, kvs_s, win_s, st_s = _sample_path(
        x_sample[:, 0], mod[1:1 + nb], wts, jnp.full((nb,), p_len), w_cmp[0], b_cmp[0],
        cache_cmp_kv[0], cache_sel_kv[0], cache_win_kv[0], state_ret[0], page_table)
    kv5 = lambda a, lead: a.reshape(lead + (2, NSA_KV_HEADS, HEAD_DIM))
    wlen = min(WINDOW, t)
    return (yp[None], ys[:, None, :],
            kv5(kvc_p, (1, 1, t)), kv5(kvs_p, (1, 1, t)), kv5(kvw_p[t - wlen:], (1, 1, wlen)), st_p[None, None],
            kv5(kvc_s, (1, nb, 1)), kv5(kvs_s, (1, nb, 1)), kv5(win_s, (1, nb, win_s.shape[1])), st_s[None])
```

```python
import functools

import jax
import jax.numpy as jnp
from jax import lax
from jax.experimental import pallas as pl
from jax.experimental.pallas import tpu as pltpu

D_MODEL = 1024
PAGE_SIZE = 128
NSA_HEADS = 8
NSA_KV_HEADS = 2
NSA_GROUP = NSA_HEADS // NSA_KV_HEADS
HEAD_DIM = 64
CMP_LEN = 32
CMP_STRIDE = 16
SEL_BLOCK = 64
N_SEL = 16
WINDOW = 512
RET_HEADS = 4
RET_DK = 128
RET_DV = 256
RET_CHUNK = 128
D_FF = 4 * D_MODEL
ROPE_THETA = 10000.0
RMS_EPS = 1e-6
GN_EPS = 1e-5
NEG_INF = -1e30
FORCE_SCORE = 1e4

NSA_Q_W = NSA_HEADS * HEAD_DIM
NSA_KV_W = 2 * NSA_KV_HEADS * HEAD_DIM
RET_QK_W = RET_HEADS * RET_DK
RET_V_W = RET_HEADS * RET_DV
N_GATES = 3 * NSA_HEADS

LANES = 128
SUBLANES = 8
VMEM_LIMIT_BYTES = 56 * 1024 * 1024

BF16 = jnp.bfloat16
F32 = jnp.float32

QPAD_W = NSA_HEADS * LANES
C_Q = 0
C_KVC = C_Q + QPAD_W
C_KVS = C_KVC + NSA_KV_W
C_KVW = C_KVS + NSA_KV_W
C_RQ = C_KVW + NSA_KV_W
C_RK = C_RQ + RET_QK_W
C_RV = C_RK + RET_QK_W
C_RG = C_RV + RET_V_W
C_GA = C_RG + RET_V_W
C_GB = C_GA + D_MODEL
C_GT = C_GB + D_MODEL
IN_PAD_W = C_GT + LANES


def _cparams(sem):
    return pltpu.CompilerParams(dimension_semantics=sem, vmem_limit_bytes=VMEM_LIMIT_BYTES)


def _vmem_full():
    return pl.BlockSpec(memory_space=pltpu.VMEM)


def _ada_kernel(c_ref, w_ref, b_ref, o_ref):
    c = c_ref[...]
    s = (c * jax.nn.sigmoid(c)).astype(BF16)
    o_ref[...] = jnp.dot(s, w_ref[...], preferred_element_type=F32) + b_ref[...]


def _ada_call(c, w_bf, b):
    rows = c.shape[0]
    n = w_bf.shape[1]
    tn = 1536
    return pl.pallas_call(
        _ada_kernel,
        grid=(n // tn,),
        in_specs=[pl.BlockSpec((rows, D_MODEL), lambda j: (0, 0)),
                  pl.BlockSpec((D_MODEL, tn), lambda j: (0, j)),
                  pl.BlockSpec((1, tn), lambda j: (0, j))],
        out_specs=pl.BlockSpec((rows, tn), lambda j: (0, j)),
        out_shape=jax.ShapeDtypeStruct((rows, n), F32),
        compiler_params=_cparams(("arbitrary",)),
        name="ada_mod",
    )(c, w_bf, b)


def _rope_slab(x, cos, sin_signed, half):
    lane = lax.broadcasted_iota(jnp.int32, x.shape, 1)
    if 2 * half == LANES:
        partner = pltpu.roll(x, half, 1)
    else:
        first = (lane % (2 * half)) < half
        partner = jnp.where(first, pltpu.roll(x, LANES - half, 1), pltpu.roll(x, half, 1))
    return x * cos + partner * sin_signed


def _inproj_kernel(x_ref, g_ref, sc_ref, sh_ref, w_ref, cs64_ref, cs128_ref, *out_refs, transposed):
    if transposed:
        (q_ref, kvc_ref, kvs_ref, kvw_ref, ks_ref, vsT_ref, kw_ref, vwT_ref,
         rq_ref, rk_ref, rv_ref, sg_ref, ga_ref, gb_ref, gt_ref) = out_refs
    else:
        (q_ref, kvc_ref, kvs_ref, kvw_ref,
         rq_ref, rk_ref, rv_ref, sg_ref, ga_ref, gb_ref, gt_ref) = out_refs
    x = x_ref[...]
    ms = jnp.mean(x * x, axis=-1, keepdims=True)
    h = (x * lax.rsqrt(ms + RMS_EPS) * g_ref[...]) * (1.0 + sc_ref[...]) + sh_ref[...]
    hb = h.astype(BF16)
    c64 = cs64_ref[:, :LANES]
    s64 = cs64_ref[:, LANES:]
    c128 = cs128_ref[:, :LANES]
    s128 = cs128_ref[:, LANES:]

    def proj(c0, width):
        return jnp.dot(hb, w_ref[:, c0:c0 + width], preferred_element_type=F32)

    zq = proj(C_Q, QPAD_W)
    for n in range(NSA_HEADS):
        slab = _rope_slab(zq[:, n * LANES:(n + 1) * LANES], c64, s64, HEAD_DIM // 2)
        q_ref[:, n * LANES:(n + 1) * LANES] = (slab * (HEAD_DIM ** -0.5)).astype(BF16)

    kv_refs = (kvc_ref, kvs_ref, kvw_ref)
    for i, c0 in enumerate((C_KVC, C_KVS, C_KVW)):
        z = proj(c0, NSA_KV_W)
        k = _rope_slab(z[:, :LANES], c64, s64, HEAD_DIM // 2)
        v = z[:, LANES:]
        kv_refs[i][:, :LANES] = k
        kv_refs[i][:, LANES:] = v
        if transposed and i > 0:
            k_ref, vT_ref = ((ks_ref, vsT_ref), (kw_ref, vwT_ref))[i - 1]
            k_ref[...] = k.astype(BF16)
            vT = v.T.astype(BF16)
            for t in range(vT_ref.shape[0]):
                vT_ref[t] = vT[:, t * LANES:(t + 1) * LANES]

    zr = proj(C_RQ, RET_QK_W)
    for n in range(RET_HEADS):
        rq_ref[:, n * LANES:(n + 1) * LANES] = _rope_slab(
            zr[:, n * LANES:(n + 1) * LANES], c128, s128, RET_DK // 2).astype(BF16)
    zr = proj(C_RK, RET_QK_W)
    for n in range(RET_HEADS):
        rk_ref[:, n * LANES:(n + 1) * LANES] = (_rope_slab(
            zr[:, n * LANES:(n + 1) * LANES], c128, s128, RET_DK // 2) * (RET_DK ** -0.5)).astype(BF16)
    rv_ref[...] = proj(C_RV, RET_V_W).astype(BF16)
    z = proj(C_RG, RET_V_W)
    sg_ref[...] = (z * jax.nn.sigmoid(z)).astype(BF16)
    ga_ref[...] = jax.nn.sigmoid(proj(C_GA, D_MODEL)).astype(BF16)
    gb_ref[...] = jax.nn.sigmoid(proj(C_GB, D_MODEL)).astype(BF16)
    gt = jax.nn.sigmoid(proj(C_GT, LANES))
    if transposed:
        gt_ref[...] = gt.T
    else:
        gt_ref[...] = gt


def _inproj_call(x, g, sc, sh, w_pad, cs64, cs128, *, tm, transposed):
    m = x.shape[0]
    per_row = sc.shape[0] != 1
    row = lambda i: (i, 0)
    mod_spec = pl.BlockSpec((tm, D_MODEL), row) if per_row else pl.BlockSpec((1, D_MODEL), lambda i: (0, 0))
    nt = tm // LANES
    shp = lambda w, dt: jax.ShapeDtypeStruct((m, w), dt)
    bs = lambda w: pl.BlockSpec((tm, w), row)
    out_shape = [shp(QPAD_W, BF16), shp(NSA_KV_W, F32), shp(NSA_KV_W, F32), shp(NSA_KV_W, F32)]
    out_specs = [bs(QPAD_W), bs(NSA_KV_W), bs(NSA_KV_W), bs(NSA_KV_W)]
    if transposed:
        for _ in range(2):
            out_shape += [shp(LANES, BF16), jax.ShapeDtypeStruct((m // LANES, LANES, LANES), BF16)]
            out_specs += [bs(LANES), pl.BlockSpec((nt, LANES, LANES), lambda i: (i, 0, 0))]
    out_shape += [shp(RET_QK_W, BF16), shp(RET_QK_W, BF16), shp(RET_V_W, BF16), shp(RET_V_W, BF16),
                  shp(D_MODEL, BF16), shp(D_MODEL, BF16)]
    out_specs += [bs(RET_QK_W), bs(RET_QK_W), bs(RET_V_W), bs(RET_V_W), bs(D_MODEL), bs(D_MODEL)]
    if transposed:
        out_shape.append(jax.ShapeDtypeStruct((LANES, m), F32))
        out_specs.append(pl.BlockSpec((LANES, tm), lambda i: (0, i)))
    else:
        out_shape.append(shp(LANES, F32))
        out_specs.append(bs(LANES))
    return pl.pallas_call(
        functools.partial(_inproj_kernel, transposed=transposed),
        grid=(m // tm,),
        in_specs=[pl.BlockSpec((tm, D_MODEL), row),
                  pl.BlockSpec((1, D_MODEL), lambda i: (0, 0)),
                  mod_spec, mod_spec,
                  _vmem_full(),
                  pl.BlockSpec((tm, 2 * LANES), row),
                  pl.BlockSpec((tm, 2 * LANES), row)],
        out_specs=out_specs,
        out_shape=out_shape,
        compiler_params=_cparams(("arbitrary",)),
        name="inproj_t" if transposed else "inproj_n",
    )(x, g, sc, sh, w_pad, cs64, cs128)


def _rope_tables(pos, d):
    half = d // 2
    inv = ROPE_THETA ** (-jnp.arange(0, d, 2, dtype=F32) / d)
    ang = pos.astype(F32)[:, None] * inv[None, :]
    cos = jnp.cos(ang)
    sin = jnp.sin(ang)
    reps = LANES // d
    cos_t = jnp.tile(jnp.concatenate([cos, cos], axis=1), (1, reps))
    sin_t = jnp.tile(jnp.concatenate([-sin, sin], axis=1), (1, reps))
    return jnp.concatenate([cos_t, sin_t], axis=1)


def _pad_in_weights(w_in):
    w = w_in.astype(BF16)
    k = w.shape[0]
    zeros = jnp.zeros((k, HEAD_DIM), BF16)
    cols = []
    for n in range(NSA_HEADS):
        wq = w[:, n * HEAD_DIM:(n + 1) * HEAD_DIM]
        cols += [wq, zeros] if n // NSA_GROUP == 0 else [zeros, wq]
    o = NSA_Q_W
    cols.append(w[:, o:o + 3 * NSA_KV_W])
    o += 3 * NSA_KV_W
    gates = w[:, o:o + N_GATES]
    o += N_GATES
    cols.append(w[:, o:])
    cols.append(gates)
    cols.append(jnp.zeros((k, LANES - N_GATES), BF16))
    return jnp.concatenate(cols, axis=1)


def _compress_kernel(x_ref, xn_ref, w1_ref, w2_ref, b_ref, o_ref, *, n_valid):
    g = x_ref.shape[0]
    x = x_ref[...]
    first = jnp.sum(x * w1_ref[...][None], axis=1)
    second = jnp.sum(x * w2_ref[...][None], axis=1)
    nxt = jnp.sum(xn_ref[0] * w2_ref[...], axis=0, keepdims=True)
    row = lax.broadcasted_iota(jnp.int32, first.shape, 0)
    shifted = jnp.where(row == g - 1, nxt, pltpu.roll(second, g - 1, 0))
    cmp = first + shifted + b_ref[...]
    n = pl.program_id(0) * g + row
    o_ref[...] = jnp.where(n < n_valid, cmp, 0.0)


def _cmp_weights(w_cmp, b_cmp):
    w = jnp.transpose(w_cmp.astype(F32), (2, 0, 1, 3)).reshape(CMP_LEN, NSA_KV_W)
    return w[:CMP_STRIDE], w[CMP_STRIDE:], b_cmp.astype(F32).reshape(1, NSA_KV_W)


def _compress_call(kvc, w1, w2, b):
    t = kvc.shape[0]
    ng = t // CMP_STRIDE
    g = min(ng, 128)
    x3 = kvc.reshape(ng, CMP_STRIDE, NSA_KV_W)
    return pl.pallas_call(
        functools.partial(_compress_kernel, n_valid=ng - 1),
        grid=(ng // g,),
        in_specs=[pl.BlockSpec((g, CMP_STRIDE, NSA_KV_W), lambda i: (i, 0, 0)),
                  pl.BlockSpec((1, CMP_STRIDE, NSA_KV_W), lambda i: (jnp.minimum((i + 1) * g, ng - 1), 0, 0)),
                  pl.BlockSpec((CMP_STRIDE, NSA_KV_W), lambda i: (0, 0)),
                  pl.BlockSpec((CMP_STRIDE, NSA_KV_W), lambda i: (0, 0)),
                  pl.BlockSpec((1, NSA_KV_W), lambda i: (0, 0))],
        out_specs=pl.BlockSpec((g, NSA_KV_W), lambda i: (i, 0)),
        out_shape=jax.ShapeDtypeStruct((ng, NSA_KV_W), F32),
        compiler_params=_cparams(("arbitrary",)),
        name="compress",
    )(x3, x3, w1, w2, b)


def _slab_major(a, nsel):
    r = SEL_BLOCK // CMP_STRIDE
    return a.reshape(nsel, r, a.shape[-1]).transpose(1, 0, 2).reshape(nsel * r, a.shape[-1])


def _nt_dot(a, b):
    return lax.dot_general(a, b, (((1,), (1,)), ((), ())), preferred_element_type=F32)


def _pairs_to_natural(o_ref, accs, h):
    for c in range(NSA_GROUP // 2):
        both = jnp.concatenate([accs[2 * c], accs[2 * c + 1]], axis=0)
        col = (NSA_GROUP // 2) * h + c
        o_ref[:, col * LANES:(col + 1) * LANES] = both.T


def _cmp_attn_kernel(q_ref, ck_ref, cvT_ref, blk_ref, gt_ref, oc_ref, sel_ref, *, tq, nsel, fixed_pos):
    ncp = 4 * nsel
    if fixed_pos is None:
        qpos = pl.program_id(0) * tq + lax.broadcasted_iota(jnp.int32, (1, tq), 1)
    else:
        qpos = jnp.full((1, tq), fixed_pos, jnp.int32)
    valid_c = (CMP_STRIDE * blk_ref[...] + (CMP_LEN - 1)) <= qpos
    jj = lax.broadcasted_iota(jnp.int32, (nsel, tq), 0)
    qblk = qpos // SEL_BLOCK
    valid_s = jj * SEL_BLOCK <= qpos
    forced = (jj == 0) | (jj == qblk) | (jj == qblk - 1)
    ck = ck_ref[0]
    for h in range(NSA_KV_HEADS):
        imp = jnp.zeros((ncp, tq), F32)
        accs = []
        cv = cvT_ref[0, h * HEAD_DIM:(h + 1) * HEAD_DIM, :]
        for g in range(NSA_GROUP):
            n = NSA_GROUP * h + g
            s = _nt_dot(ck, q_ref[:, n * LANES:(n + 1) * LANES])
            s = jnp.where(valid_c, s, NEG_INF)
            e = jnp.exp(s - jnp.max(s, axis=0, keepdims=True))
            p = jnp.where(valid_c, e * (1.0 / jnp.sum(e, axis=0, keepdims=True)), 0.0)
            imp = imp + p
            o = jnp.dot(cv, p.astype(BF16), preferred_element_type=F32)
            accs.append(o * gt_ref[3 * n:3 * n + 1, :])
        _pairs_to_natural(oc_ref, accs, h)
        p3 = imp[3 * nsel:]
        prev = jnp.where(jj == 0, 0.0, pltpu.roll(p3, 1, 0))
        pslc = imp[:nsel] + imp[nsel:2 * nsel] + imp[2 * nsel:3 * nsel] + p3 + prev
        work = jnp.where(valid_s, pslc + jnp.where(forced, FORCE_SCORE, 0.0), NEG_INF)
        sel = jnp.zeros((nsel, tq), jnp.bool_)
        for _ in range(min(N_SEL, nsel)):
            cand = work == jnp.max(work, axis=0, keepdims=True)
            idx = jnp.min(jnp.where(cand, jj, nsel), axis=0, keepdims=True)
            hit = jj == idx
            sel = sel | hit
            work = jnp.where(hit, -jnp.inf, work)
        sel_ref[h] = jnp.where(sel, 0.0, NEG_INF)


def _cmp_attn_call(qpad, ck, cvT, gT, *, tq, fixed_pos=None):
    t = qpad.shape[0]
    nsel = ck.shape[1] // (SEL_BLOCK // CMP_STRIDE)
    per_step = ck.shape[0] != 1
    kv_idx = (lambda i: (i, 0, 0)) if per_step else (lambda i: (0, 0, 0))
    r = jnp.arange(4 * nsel, dtype=jnp.int32)
    blk = jnp.broadcast_to(((SEL_BLOCK // CMP_STRIDE) * (r % nsel) + r // nsel)[:, None], (4 * nsel, tq))
    return pl.pallas_call(
        functools.partial(_cmp_attn_kernel, tq=tq, nsel=nsel, fixed_pos=fixed_pos),
        grid=(t // tq,),
        in_specs=[pl.BlockSpec((tq, QPAD_W), lambda i: (i, 0)),
                  pl.BlockSpec((1, 4 * nsel, LANES), kv_idx),
                  pl.BlockSpec((1, LANES, 4 * nsel), kv_idx),
                  pl.BlockSpec((4 * nsel, tq), lambda i: (0, 0)),
                  pl.BlockSpec((LANES, tq), lambda i: (0, i))],
        out_specs=[pl.BlockSpec((tq, NSA_Q_W), lambda i: (i, 0)),
                   pl.BlockSpec((NSA_KV_HEADS, nsel, tq), lambda i: (0, 0, i))],
        out_shape=[jax.ShapeDtypeStruct((t, NSA_Q_W), F32),
                   jax.ShapeDtypeStruct((NSA_KV_HEADS, nsel, t), F32)],
        compiler_params=_cparams(("arbitrary",)),
        name="cmp_attn_topk_n" if per_step else "cmp_attn_topk_t",
    )(qpad, ck, cvT, blk, gT)


def _stack_q(q_ref, h):
    return jnp.concatenate([q_ref[:, (NSA_GROUP * h + g) * LANES:(NSA_GROUP * h + g + 1) * LANES]
                            for g in range(NSA_GROUP)], axis=0)


def _gate_row(gt_ref, h, branch):
    return jnp.concatenate([gt_ref[3 * (NSA_GROUP * h + g) + branch:3 * (NSA_GROUP * h + g) + branch + 1, :]
                            for g in range(NSA_GROUP)], axis=1)


def _vT_tile(vT_ref, first, count, h):
    v = vT_ref[pl.ds(first, count)]
    return jnp.concatenate([v[t, h * HEAD_DIM:(h + 1) * HEAD_DIM, :] for t in range(count)], axis=1)


def _sel_attn_kernel(q_ref, k_ref, vT_ref, sel_ref, gt_ref, o_ref, *, tq, tk):
    s0 = pl.program_id(0) * tq
    kd = s0 // tk
    nb = tk // SEL_BLOCK
    r = NSA_GROUP * tq
    qpos = s0 + lax.broadcasted_iota(jnp.int32, (1, tq), 1)
    qpos = jnp.concatenate([qpos] * NSA_GROUP, axis=1)
    krow = lax.broadcasted_iota(jnp.int32, (tk, r), 0)
    qhs = [_stack_q(q_ref, h) for h in range(NSA_KV_HEADS)]

    def step(kt, carry, causal):
        kb = k_ref[pl.ds(pl.multiple_of(kt * tk, tk), tk), :]
        out = []
        for h in range(NSA_KV_HEADS):
            m, l, acc = carry[h]
            s = _nt_dot(kb, qhs[h])
            rows = sel_ref[h, pl.ds(pl.multiple_of(kt * nb, nb), nb), :]
            bias = jnp.concatenate([jnp.broadcast_to(rows[b:b + 1], (SEL_BLOCK, tq)) for b in range(nb)], axis=0)
            s = s + jnp.concatenate([bias] * NSA_GROUP, axis=1)
            if causal:
                s = jnp.where(kt * tk + krow <= qpos, s, NEG_INF)
            m_new = jnp.maximum(m, jnp.max(s, axis=0, keepdims=True))
            p = jnp.exp(s - m_new)
            alpha = jnp.exp(m - m_new)
            l = alpha * l + jnp.sum(p, axis=0, keepdims=True)
            pv = jnp.dot(_vT_tile(vT_ref, kt * (tk // LANES), tk // LANES, h), p.astype(BF16),
                         preferred_element_type=F32)
            out.append((m_new, l, alpha * acc + pv))
        return tuple(out)

    init = tuple((jnp.full((1, r), NEG_INF, F32), jnp.zeros((1, r), F32), jnp.zeros((HEAD_DIM, r), F32))
                 for _ in range(NSA_KV_HEADS))
    carry = lax.fori_loop(0, kd, functools.partial(step, causal=False), init)
    final = step(kd, carry, True)
    for h in range(NSA_KV_HEADS):
        m, l, acc = final[h]
        o = acc * (1.0 / l) * _gate_row(gt_ref, h, 1)
        _pairs_to_natural(o_ref, [o[:, g * tq:(g + 1) * tq] for g in range(NSA_GROUP)], h)


def _sel_attn_call(qpad, ks, vsT, selb, gT, *, tq, tk):
    t = qpad.shape[0]
    nsel = t // SEL_BLOCK
    return pl.pallas_call(
        functools.partial(_sel_attn_kernel, tq=tq, tk=tk),
        grid=(t // tq,),
        in_specs=[pl.BlockSpec((tq, QPAD_W), lambda i: (i, 0)),
                  _vmem_full(), _vmem_full(),
                  pl.BlockSpec((NSA_KV_HEADS, nsel, tq), lambda i: (0, 0, i)),
                  pl.BlockSpec((LANES, tq), lambda i: (0, i))],
        out_specs=pl.BlockSpec((tq, NSA_Q_W), lambda i: (i, 0)),
        out_shape=jax.ShapeDtypeStruct((t, NSA_Q_W), F32),
        compiler_params=_cparams(("arbitrary",)),
        name="sel_attn",
    )(qpad, ks, vsT, selb, gT)


def _win_attn_kernel(q_ref, k_ref, vT_ref, gt_ref, o_ref, *, tq):
    band = WINDOW + tq
    s0 = pl.program_id(0) * tq
    start = pl.multiple_of(jnp.maximum(s0 - WINDOW, 0), LANES)
    r = NSA_GROUP * tq
    qpos = s0 + lax.broadcasted_iota(jnp.int32, (1, tq), 1)
    qpos = jnp.concatenate([qpos] * NSA_GROUP, axis=1)
    kp = start + lax.broadcasted_iota(jnp.int32, (band, r), 0)
    valid = (kp <= qpos) & (kp > qpos - WINDOW)
    kb = k_ref[pl.ds(start, band), :]
    for h in range(NSA_KV_HEADS):
        s = jnp.where(valid, _nt_dot(kb, _stack_q(q_ref, h)), NEG_INF)
        e = jnp.exp(s - jnp.max(s, axis=0, keepdims=True))
        p = e * (1.0 / jnp.sum(e, axis=0, keepdims=True))
        o = jnp.dot(_vT_tile(vT_ref, start // LANES, band // LANES, h), p.astype(BF16),
                    preferred_element_type=F32)
        o = o * _gate_row(gt_ref, h, 2)
        _pairs_to_natural(o_ref, [o[:, g * tq:(g + 1) * tq] for g in range(NSA_GROUP)], h)


def _win_attn_call(qpad, kw, vwT, gT, *, tq):
    t = qpad.shape[0]
    return pl.pallas_call(
        functools.partial(_win_attn_kernel, tq=tq),
        grid=(t // tq,),
        in_specs=[pl.BlockSpec((tq, QPAD_W), lambda i: (i, 0)),
                  _vmem_full(), _vmem_full(),
                  pl.BlockSpec((LANES, tq), lambda i: (0, i))],
        out_specs=pl.BlockSpec((tq, NSA_Q_W), lambda i: (i, 0)),
        out_shape=jax.ShapeDtypeStruct((t, NSA_Q_W), F32),
        compiler_params=_cparams(("arbitrary",)),
        name="win_attn",
    )(qpad, kw, vwT, gT)


def _ret_tables(c):
    lg = jnp.log1p(-jnp.exp2(-5.0 - jnp.arange(RET_HEADS, dtype=F32)))
    i = jnp.arange(c, dtype=F32)
    diff = i[:, None] - i[None, :]
    dmat = jnp.where(diff >= 0, jnp.exp(jnp.maximum(diff, 0.0)[None] * lg[:, None, None]), 0.0)
    qdec = jnp.exp((i[None, :] + 1.0) * lg[:, None])
    kdec = jnp.exp((c - 1.0 - i)[None, :] * lg[:, None])
    gc = jnp.exp(c * lg)
    return (dmat,
            jnp.broadcast_to(qdec[:, :, None], (RET_HEADS, c, RET_DV)),
            jnp.broadcast_to(kdec[:, :, None], (RET_HEADS, c, RET_DK)),
            jnp.broadcast_to(gc[:, None, None], (RET_HEADS, SUBLANES, RET_DV)))


def _group_norm_gate(o, sg, gnw, gnb):
    mu = jnp.mean(o, axis=-1, keepdims=True)
    d = o - mu
    var = jnp.mean(d * d, axis=-1, keepdims=True)
    return sg.astype(F32) * (d * lax.rsqrt(var + GN_EPS) * gnw + gnb)


def _ret_kernel(rq_ref, rk_ref, rv_ref, sg_ref, dm_ref, qd_ref, kd_ref, gc_ref, gnw_ref, gnb_ref,
                y_ref, st_ref, s_scr, *, tr):
    @pl.when(pl.program_id(0) == 0)
    def _():
        s_scr[...] = jnp.zeros_like(s_scr)

    c = RET_CHUNK
    for ci in range(tr // c):
        rows = slice(ci * c, (ci + 1) * c)
        for h in range(RET_HEADS):
            q = rq_ref[rows, h * RET_DK:(h + 1) * RET_DK]
            k = rk_ref[rows, h * RET_DK:(h + 1) * RET_DK]
            v = rv_ref[rows, h * RET_DV:(h + 1) * RET_DV]
            att = _nt_dot(q, k) * dm_ref[h]
            s_old = s_scr[h]
            o = jnp.dot(att.astype(BF16), v, preferred_element_type=F32)
            o = o + jnp.dot(q, s_old.astype(BF16), preferred_element_type=F32) * qd_ref[h]
            kdT = (k.astype(F32) * kd_ref[h]).T.astype(BF16)
            s_scr[h] = gc_ref[h][:1] * s_old + jnp.dot(kdT, v, preferred_element_type=F32)
            cols = slice(h * RET_DV, (h + 1) * RET_DV)
            y_ref[rows, cols] = _group_norm_gate(o, sg_ref[rows, cols], gnw_ref[:, cols], gnb_ref[:, cols]).astype(BF16)
    st_ref[...] = s_scr[...]


def _ret_call(rq, rk, rv, sg, gnw, gnb, *, tr):
    t = rq.shape[0]
    dm, qd, kd, gc = _ret_tables(RET_CHUNK)
    row = lambda i: (i, 0)
    full3 = lambda shape: pl.BlockSpec(shape, lambda i: (0, 0, 0))
    return pl.pallas_call(
        functools.partial(_ret_kernel, tr=tr),
        grid=(t // tr,),
        in_specs=[pl.BlockSpec((tr, RET_QK_W), row), pl.BlockSpec((tr, RET_QK_W), row),
                  pl.BlockSpec((tr, RET_V_W), row), pl.BlockSpec((tr, RET_V_W), row),
                  full3(dm.shape), full3(qd.shape), full3(kd.shape), full3(gc.shape),
                  pl.BlockSpec((1, RET_V_W), lambda i: (0, 0)), pl.BlockSpec((1, RET_V_W), lambda i: (0, 0))],
        out_specs=[pl.BlockSpec((tr, RET_V_W), row), full3((RET_HEADS, RET_DK, RET_DV))],
        out_shape=[jax.ShapeDtypeStruct((t, RET_V_W), BF16),
                   jax.ShapeDtypeStruct((RET_HEADS, RET_DK, RET_DV), F32)],
        scratch_shapes=[pltpu.VMEM((RET_HEADS, RET_DK, RET_DV), F32)],
        compiler_params=_cparams(("arbitrary",)),
        name="retention",
    )(rq, rk, rv, sg, dm, qd, kd, gc, gnw, gnb)


def _rms(x, g):
    return x * lax.rsqrt(jnp.mean(x * x, axis=-1, keepdims=True) + RMS_EPS) * g


def _post_kernel(x_ref, oc_ref, os_ref, ow_ref, yr_ref, ga_ref, gb_ref,
                 gta_ref, shf_ref, scf_ref, gtf_ref, gm_ref, gf_ref,
                 wa_ref, wb_ref, wo_ref, wu_ref, wd_ref, y_ref, *, ff_chunk):
    o_nsa = (oc_ref[...] + os_ref[...] + ow_ref[...]).astype(BF16)
    y_a = jnp.dot(o_nsa, wa_ref[...], preferred_element_type=F32)
    y_b = jnp.dot(yr_ref[...], wb_ref[...], preferred_element_type=F32)
    merged = ga_ref[...].astype(F32) * y_a + gb_ref[...].astype(F32) * y_b
    mixed = jnp.dot(merged.astype(BF16), wo_ref[...], preferred_element_type=F32)
    x1 = x_ref[...] + gta_ref[...] * mixed
    h2 = (_rms(x1, gm_ref[...]) * (1.0 + scf_ref[...]) + shf_ref[...]).astype(BF16)
    mlp = jnp.zeros_like(x1)
    for c0 in range(0, D_FF, ff_chunk):
        u = jnp.maximum(jnp.dot(h2, wu_ref[:, c0:c0 + ff_chunk], preferred_element_type=F32), 0.0)
        mlp = mlp + jnp.dot((u * u).astype(BF16), wd_ref[c0:c0 + ff_chunk, :], preferred_element_type=F32)
    x2 = x1 + gtf_ref[...] * mlp
    y_ref[...] = _rms(x2, gf_ref[...])


def _post_call(x, oc, osel, ow, yr, ga, gb, gta, shf, scf, gtf, gm, gf, wa, wb, wo, wu, wd, *, tm):
    m = x.shape[0]
    per_row = gta.shape[0] != 1
    row = lambda i: (i, 0)
    one = lambda i: (0, 0)
    mod_spec = pl.BlockSpec((tm, D_MODEL), row) if per_row else pl.BlockSpec((1, D_MODEL), one)
    return pl.pallas_call(
        functools.partial(_post_kernel, ff_chunk=1024),
        grid=(m // tm,),
        in_specs=[pl.BlockSpec((tm, D_MODEL), row),
                  pl.BlockSpec((tm, NSA_Q_W), row), pl.BlockSpec((tm, NSA_Q_W), row), pl.BlockSpec((tm, NSA_Q_W), row),
                  pl.BlockSpec((tm, RET_V_W), row), pl.BlockSpec((tm, D_MODEL), row), pl.BlockSpec((tm, D_MODEL), row),
                  mod_spec, mod_spec, mod_spec, mod_spec,
                  pl.BlockSpec((1, D_MODEL), one), pl.BlockSpec((1, D_MODEL), one),
                  _vmem_full(), _vmem_full(), _vmem_full(), _vmem_full(), _vmem_full()],
        out_specs=pl.BlockSpec((tm, D_MODEL), row),
        out_shape=jax.ShapeDtypeStruct((m, D_MODEL), F32),
        compiler_params=_cparams(("arbitrary",)),
        name="post_t" if not per_row else "post_n",
    )(x, oc, osel, ow, yr, ga, gb, gta, shf, scf, gtf, gm, gf, wa, wb, wo, wu, wd)


def _prompt_path(x, mod, wts, pos, w_cmp, b_cmp):
    t = x.shape[0]
    sh_a, sc_a, gt_a, sh_f, sc_f, gt_f = [mod[:, i * D_MODEL:(i + 1) * D_MODEL] for i in range(6)]
    (q, kvc, kvs, kvw, ks, vsT, kw, vwT, rq, rk, rv, sg, ga, gb, gT) = _inproj_call(
        x, wts["norm_mix"], sc_a, sh_a, wts["w_in"], _rope_tables(pos, HEAD_DIM), _rope_tables(pos, RET_DK),
        tm=512, transposed=True)
    w1, w2, b = _cmp_weights(w_cmp, b_cmp)
    cmp = _compress_call(kvc, w1, w2, b)
    nsel = t // SEL_BLOCK
    ck = _slab_major(cmp[:, :LANES], nsel).astype(BF16)
    cvT = _slab_major(cmp[:, LANES:], nsel).T.astype(BF16)
    oc, selb = _cmp_attn_call(q, ck[None], cvT[None], gT, tq=128)
    osel = _sel_attn_call(q, ks, vsT, selb, gT, tq=128, tk=512)
    ow = _win_attn_call(q, kw, vwT, gT, tq=128)
    yr, state = _ret_call(rq, rk, rv, sg, wts["gn_w"], wts["gn_b"], tr=512)
    y = _post_call(x, oc, osel, ow, yr, ga, gb, gt_a, sh_f, sc_f, gt_f, wts["norm_mlp"], wts["norm_final"],
                   wts["w_a"], wts["w_b"], wts["w_o"], wts["w_u"], wts["w_d"], tm=256)
    return y, kvc, kvs, kvw, state


PAGES_PER_STEP = 16
GROUPS_PER_PAGE = PAGE_SIZE // CMP_STRIDE


def _paged_compress_kernel(pt_ref, *refs):
    page_refs = refs[:PAGES_PER_STEP]
    nxt_ref, new_ref, w1_ref, w2_ref, b_ref, o_ref = refs[PAGES_PER_STEP:]
    x = jnp.concatenate([r[0] for r in page_refs], axis=0)
    g = x.shape[0]
    first = jnp.sum(x * w1_ref[...][None], axis=1)
    second = jnp.sum(x * w2_ref[...][None], axis=1)
    nxt_page = jnp.sum(nxt_ref[0, 0] * w2_ref[...], axis=0, keepdims=True)
    nxt_new = new_ref[0] * w2_ref[0:1, :]
    last = pl.program_id(1) == pl.num_programs(1) - 1
    nxt = jnp.where(last, nxt_new, nxt_page)
    row = lax.broadcasted_iota(jnp.int32, first.shape, 0)
    shifted = jnp.where(row == g - 1, nxt, pltpu.roll(second, g - 1, 0))
    o_ref[0] = first + shifted + b_ref[...]


def _paged_compress_call(cache, page_table, new_row, w1, w2, b):
    nb, npages = page_table.shape
    steps = npages // PAGES_PER_STEP
    c4 = cache.reshape(cache.shape[0], GROUPS_PER_PAGE, CMP_STRIDE, NSA_KV_W)
    page_spec = lambda k: pl.BlockSpec((1, GROUPS_PER_PAGE, CMP_STRIDE, NSA_KV_W),
                                       lambda bi, i, pt: (pt[bi, PAGES_PER_STEP * i + k], 0, 0, 0))
    nxt_spec = pl.BlockSpec((1, GROUPS_PER_PAGE, CMP_STRIDE, NSA_KV_W),
                            lambda bi, i, pt: (pt[bi, jnp.minimum(PAGES_PER_STEP * (i + 1), npages - 1)], 0, 0, 0))
    const = lambda shape: pl.BlockSpec(shape, lambda bi, i, pt: (0, 0))
    g = PAGES_PER_STEP * GROUPS_PER_PAGE
    return pl.pallas_call(
        _paged_compress_kernel,
        grid_spec=pltpu.PrefetchScalarGridSpec(
            num_scalar_prefetch=1,
            grid=(nb, steps),
            in_specs=[page_spec(k) for k in range(PAGES_PER_STEP)] + [
                nxt_spec,
                pl.BlockSpec((1, 1, NSA_KV_W), lambda bi, i, pt: (bi, 0, 0)),
                const((CMP_STRIDE, NSA_KV_W)), const((CMP_STRIDE, NSA_KV_W)), const((1, NSA_KV_W))],
            out_specs=pl.BlockSpec((1, g, NSA_KV_W), lambda bi, i, pt: (bi, i, 0)),
        ),
        out_shape=jax.ShapeDtypeStruct((nb, npages * GROUPS_PER_PAGE, NSA_KV_W), F32),
        compiler_params=_cparams(("arbitrary", "arbitrary")),
        name="paged_compress",
    )(page_table, *([c4] * (PAGES_PER_STEP + 1)), new_row, w1, w2, b)


def _softmax_lanes(s):
    e = jnp.exp(s - jnp.max(s, axis=-1, keepdims=True))
    return e * (1.0 / jnp.sum(e, axis=-1, keepdims=True))


def _gather_blocks_kernel(idx_ref, pt_ref, *refs):
    o_ref = refs[N_SEL]
    for k in range(N_SEL):
        o_ref[0, 0, k] = refs[k][0]


def _gather_blocks_call(idx, page_table, cache5):
    nb = idx.shape[0]
    bpp = PAGE_SIZE // SEL_BLOCK
    n_past_blk = page_table.shape[1] * bpp
    tail = cache5.shape[2:]

    def blk_map(k):
        def f(bi, h, idx_ref, pt_ref):
            j = jnp.minimum(idx_ref[bi, h * N_SEL + k], n_past_blk - 1)
            return (pt_ref[bi, j // bpp], j % bpp, 0, 0, 0)
        return f

    return pl.pallas_call(
        _gather_blocks_kernel,
        grid_spec=pltpu.PrefetchScalarGridSpec(
            num_scalar_prefetch=2,
            grid=(nb, NSA_KV_HEADS),
            in_specs=[pl.BlockSpec((1, SEL_BLOCK) + tail, blk_map(k)) for k in range(N_SEL)],
            out_specs=pl.BlockSpec((1, 1, N_SEL, SEL_BLOCK) + tail,
                                   lambda bi, h, idx_ref, pt_ref: (bi, h, 0, 0, 0, 0, 0)),
        ),
        out_shape=jax.ShapeDtypeStruct((nb, NSA_KV_HEADS, N_SEL, SEL_BLOCK) + tail, F32),
        compiler_params=_cparams(("arbitrary", "arbitrary")),
        name="sample_gather_blocks",
    )(idx.reshape(nb, NSA_KV_HEADS * N_SEL), page_table, *([cache5] * N_SEL))


def _sample_sel_kernel(idx_ref, q_ref, kv_ref, new_ref, g_ref, o_ref, *, n_past_blk):
    bi = pl.program_id(0)
    h = pl.program_id(1)
    kv = kv_ref[0, 0]
    nk = kv.shape[0]
    row = lax.broadcasted_iota(jnp.int32, (nk, LANES), 0)
    lane = lax.broadcasted_iota(jnp.int32, (SUBLANES, nk), 1)
    first_new = jnp.zeros((nk, LANES), jnp.int32)
    dead = jnp.zeros((SUBLANES, nk), jnp.int32)
    for kk in range(N_SEL):
        is_new = (idx_ref[bi, h * N_SEL + kk] >= n_past_blk).astype(jnp.int32)
        first_new = first_new + jnp.where(row == kk * SEL_BLOCK, is_new, 0)
        dead = dead + jnp.where((lane > kk * SEL_BLOCK) & (lane < (kk + 1) * SEL_BLOCK), is_new, 0)
    new = new_ref[0]
    k_all = jnp.where(first_new > 0, new[:, :LANES], kv[:, :LANES]).astype(BF16)
    v_all = jnp.where(first_new > 0, new[:, LANES:], kv[:, LANES:]).astype(BF16)
    s = _nt_dot(q_ref[0, 0], k_all)
    p = _softmax_lanes(jnp.where(dead > 0, NEG_INF, s))
    o_ref[0, 0] = jnp.dot(p.astype(BF16), v_all, preferred_element_type=F32) * g_ref[0, 0]


def _sample_sel_call(idx, n_past_blk, q4, kv_sel, new_row, g4):
    nb = q4.shape[0]
    bh = lambda bi, h, idx_ref: (bi, h, 0, 0)
    return pl.pallas_call(
        functools.partial(_sample_sel_kernel, n_past_blk=n_past_blk),
        grid_spec=pltpu.PrefetchScalarGridSpec(
            num_scalar_prefetch=1,
            grid=(nb, NSA_KV_HEADS),
            in_specs=[pl.BlockSpec((1, 1, SUBLANES, LANES), bh),
                      pl.BlockSpec((1, 1) + kv_sel.shape[2:], bh),
                      pl.BlockSpec((1, 1, NSA_KV_W), lambda bi, h, idx_ref: (bi, 0, 0)),
                      pl.BlockSpec((1, 1, SUBLANES, LANES), bh)],
            out_specs=pl.BlockSpec((1, 1, SUBLANES, LANES), bh),
        ),
        out_shape=jax.ShapeDtypeStruct((nb, NSA_KV_HEADS, SUBLANES, LANES), F32),
        compiler_params=_cparams(("arbitrary", "arbitrary")),
        name="sample_sel_attn",
    )(idx.reshape(nb, NSA_KV_HEADS * N_SEL), q4, kv_sel, new_row, g4)


def _sample_win_kernel(q_ref, cw_ref, new_ref, g_ref, nw_ref, o_ref):
    cw = cw_ref[0]
    w = cw.shape[0]
    row = lax.broadcasted_iota(jnp.int32, cw.shape, 0)
    nw = jnp.where(row == w - 1, new_ref[0], pltpu.roll(cw, w - 1, 0))
    nw_ref[0] = nw
    k = nw[:, :LANES].astype(BF16)
    v = nw[:, LANES:].astype(BF16)
    for h in range(NSA_KV_HEADS):
        p = _softmax_lanes(_nt_dot(q_ref[0, h], k))
        o_ref[0, h] = jnp.dot(p.astype(BF16), v, preferred_element_type=F32) * g_ref[0, h]


def _sample_win_call(q4, cache_win, new_row, g4):
    nb, w = cache_win.shape[:2]
    b4 = lambda bi: (bi, 0, 0, 0)
    b3 = lambda bi: (bi, 0, 0)
    return pl.pallas_call(
        _sample_win_kernel,
        grid=(nb,),
        in_specs=[pl.BlockSpec((1, NSA_KV_HEADS, SUBLANES, LANES), b4),
                  pl.BlockSpec((1, w, NSA_KV_W), b3),
                  pl.BlockSpec((1, 1, NSA_KV_W), b3),
                  pl.BlockSpec((1, NSA_KV_HEADS, SUBLANES, LANES), b4)],
        out_specs=[pl.BlockSpec((1, w, NSA_KV_W), b3),
                   pl.BlockSpec((1, NSA_KV_HEADS, SUBLANES, LANES), b4)],
        out_shape=[jax.ShapeDtypeStruct((nb, w, NSA_KV_W), F32),
                   jax.ShapeDtypeStruct((nb, NSA_KV_HEADS, SUBLANES, LANES), F32)],
        compiler_params=_cparams(("arbitrary",)),
        name="sample_win_attn",
    )(q4, cache_win, new_row, g4)


def _sample_ret_kernel(q_ref, k_ref, kc_ref, v_ref, sg_ref, s_ref, dec_ref, gnw_ref, gnb_ref, y_ref, so_ref):
    for h in range(RET_HEADS):
        q = q_ref[0, h]
        k = k_ref[0, h].astype(F32)
        v = v_ref[0, h].astype(F32)
        s_old = s_ref[0, h]
        gamma = dec_ref[h]
        qk = jnp.sum(q.astype(F32) * k, axis=-1, keepdims=True)
        o = qk * v + jnp.dot(q, s_old.astype(BF16), preferred_element_type=F32) * gamma
        so_ref[0, h] = gamma[:1] * s_old + kc_ref[0, h] * v[:1]
        cols = slice(h * RET_DV, (h + 1) * RET_DV)
        y_ref[0, h] = _group_norm_gate(o, sg_ref[0, h], gnw_ref[:, cols], gnb_ref[:, cols])


def _sample_ret_call(q4, k4, kc4, v4, sg4, state, gnw, gnb):
    nb = q4.shape[0]
    _, _, _, gc = _ret_tables(1)
    b4 = lambda bi: (bi, 0, 0, 0)
    spec = lambda a: pl.BlockSpec((1,) + a.shape[1:], b4)
    return pl.pallas_call(
        _sample_ret_kernel,
        grid=(nb,),
        in_specs=[spec(q4), spec(k4), spec(kc4), spec(v4), spec(sg4), spec(state),
                  pl.BlockSpec(gc.shape, lambda bi: (0, 0, 0)),
                  pl.BlockSpec((1, RET_V_W), lambda bi: (0, 0)), pl.BlockSpec((1, RET_V_W), lambda bi: (0, 0))],
        out_specs=[pl.BlockSpec((1, RET_HEADS, SUBLANES, RET_DV), b4), spec(state)],
        out_shape=[jax.ShapeDtypeStruct((nb, RET_HEADS, SUBLANES, RET_DV), F32),
                   jax.ShapeDtypeStruct(state.shape, F32)],
        compiler_params=_cparams(("arbitrary",)),
        name="sample_retention",
    )(q4, k4, kc4, v4, sg4, state, gc, gnw, gnb)


def _row0(a, rows):
    return jnp.concatenate([a[..., None, :], jnp.zeros(a.shape[:-1] + (rows - 1, a.shape[-1]), a.dtype)], axis=-2)


def _sample_path(x, mod, wts, pos, w_cmp, b_cmp, cache_cmp, cache_sel, cache_win, state, page_table):
    nb = x.shape[0]
    p_len = page_table.shape[1] * PAGE_SIZE
    sh_a, sc_a, gt_a, sh_f, sc_f, gt_f = [mod[:, i * D_MODEL:(i + 1) * D_MODEL] for i in range(6)]
    (q, kvc, kvs, kvw, rq, rk, rv, sg, ga, gb, gates) = _inproj_call(
        x, wts["norm_mix"], sc_a, sh_a, wts["w_in"], _rope_tables(pos, HEAD_DIM), _rope_tables(pos, RET_DK),
        tm=nb, transposed=False)
    w1, w2, b = _cmp_weights(w_cmp, b_cmp)
    n_phys = cache_cmp.shape[0]
    cmp_main = _paged_compress_call(cache_cmp.reshape(n_phys, PAGE_SIZE, NSA_KV_W), page_table, kvc[:, None, :], w1, w2, b)
    nc_main = cmp_main.shape[1]
    tail = jnp.concatenate([(kvc * w1[0:1] + b)[:, None, :],
                            jnp.broadcast_to(b[None], (nb, 2, NSA_KV_W))], axis=1)
    nsel = -(-(p_len // SEL_BLOCK + 1) // 32) * 32
    ncp = nsel * (SEL_BLOCK // CMP_STRIDE)
    cmp_all = jnp.concatenate([cmp_main, tail, jnp.zeros((nb, ncp - nc_main - 3, NSA_KV_W), F32)], axis=1)
    slab = jax.vmap(lambda a: _slab_major(a, nsel))
    ck = slab(cmp_all[:, :, :LANES]).astype(BF16)
    cvT = jnp.swapaxes(slab(cmp_all[:, :, LANES:]), 1, 2).astype(BF16)
    tq = LANES
    q_lane0 = _row0(q, tq).reshape(nb * tq, QPAD_W)
    gT_lane0 = jnp.transpose(_row0(gates, tq), (2, 0, 1)).reshape(LANES, nb * tq)
    oc_rows, selb = _cmp_attn_call(q_lane0, ck, cvT, gT_lane0, tq=tq, fixed_pos=p_len)
    oc = oc_rows.reshape(nb, tq, NSA_Q_W)[:, 0]
    chosen = selb.reshape(NSA_KV_HEADS, nsel, nb, tq)[:, :, :, 0] == 0.0
    order = jnp.argsort(jnp.where(chosen, 0, 1), axis=1, stable=True)[:, :N_SEL]
    idx = jnp.transpose(order, (2, 0, 1)).astype(jnp.int32)
    q4 = q.reshape(nb, NSA_KV_HEADS, NSA_GROUP, LANES)
    q4 = jnp.concatenate([q4, jnp.zeros_like(q4)], axis=2)
    g3 = gates[:, :N_GATES].reshape(nb, NSA_KV_HEADS, NSA_GROUP, 3)
    g4 = lambda br: jnp.broadcast_to(
        jnp.concatenate([g3[..., br], jnp.zeros_like(g3[..., br])], axis=2)[..., None], (nb, NSA_KV_HEADS, SUBLANES, LANES))
    kv_sel = _gather_blocks_call(idx, page_table, cache_sel).reshape(nb, NSA_KV_HEADS, N_SEL * SEL_BLOCK, NSA_KV_W)
    os_raw = _sample_sel_call(idx, p_len // SEL_BLOCK, q4, kv_sel, kvs[:, None, :], g4(1))
    new_win, ow_raw = _sample_win_call(q4, cache_win.reshape(nb, -1, NSA_KV_W), kvw[:, None, :], g4(2))

    def natural(raw):
        parts = [raw[:, h, :NSA_GROUP, h * HEAD_DIM:(h + 1) * HEAD_DIM] for h in range(NSA_KV_HEADS)]
        return jnp.concatenate(parts, axis=1).reshape(nb, NSA_Q_W)

    rq4 = _row0(rq.reshape(nb, RET_HEADS, RET_DK), SUBLANES)
    rk3 = rk.reshape(nb, RET_HEADS, RET_DK)
    y_raw, new_state = _sample_ret_call(
        rq4, _row0(rk3, SUBLANES), rk3.astype(F32)[..., None], _row0(rv.reshape(nb, RET_HEADS, RET_DV), SUBLANES),
        _row0(sg.reshape(nb, RET_HEADS, RET_DV), SUBLANES), state, wts["gn_w"], wts["gn_b"])
    yr = y_raw[:, :, 0, :].reshape(nb, RET_V_W).astype(BF16)
    y = _post_call(x, oc, natural(os_raw), natural(ow_raw), yr, ga, gb, gt_a, sh_f, sc_f, gt_f,
                   wts["norm_mlp"], wts["norm_final"], wts["w_a"], wts["w_b"], wts["w_o"], wts["w_u"], wts["w_d"], tm=nb)
    return y, kvc, kvs, new_win, new_state


def kernel(x_prompt, x_sample, c_prompt, c_sample, cache_cmp_kv, cache_sel_kv, cache_win_kv, state_ret, page_table,
           norm_mix, norm_mlp, norm_final, w_ada, b_ada, w_in, w_cmp, b_cmp, ret_gn_w, ret_gn_b,
           w_branch_nsa, w_branch_ret, w_out, w_up, w_down):
    assert x_prompt.shape[0] == 1 and x_sample.shape[1] == 1 and norm_mix.shape[0] == 1
    t = x_prompt.shape[1]
    nb = x_sample.shape[0]
    p_len = page_table.shape[1] * PAGE_SIZE
    wts = dict(norm_mix=norm_mix[0][None], norm_mlp=norm_mlp[0][None], norm_final=norm_final[None],
               w_in=_pad_in_weights(w_in[0]), gn_w=ret_gn_w[0][None], gn_b=ret_gn_b[0][None],
               w_a=w_branch_nsa[0].astype(BF16), w_b=w_branch_ret[0].astype(BF16), w_o=w_out[0].astype(BF16),
               w_u=w_up[0].astype(BF16), w_d=w_down[0].astype(BF16))
    c_all = jnp.concatenate([c_prompt, c_sample], axis=0)
    c_all = jnp.pad(c_all, ((0, (-c_all.shape[0]) % SUBLANES), (0, 0)))
    mod = _ada_call(c_all, w_ada[0].astype(BF16), b_ada[0][None])
    yp, kvc_p, kvs_p, kvw_p, st_p = _prompt_path(x_prompt[0], mod[:1], wts, jnp.arange(t), w_cmp[0], b_cmp[0])
    ys, kvc_s, kvs_s, win_s, st_s = _sample_path(
        x_sample[:, 0], mod[1:1 + nb], wts, jnp.full((nb,), p_len), w_cmp[0], b_cmp[0],
        cache_cmp_kv[0], cache_sel_kv[0], cache_win_kv[0], state_ret[0], page_table)
    kv5 = lambda a, lead: a.reshape(lead + (2, NSA_KV_HEADS, HEAD_DIM))
    wlen = min(WINDOW, t)
    return (yp[None], ys[:, None, :],
            kv5(kvc_p, (1, 1, t)), kv5(kvs_p, (1, 1, t)), kv5(kvw_p[t - wlen:], (1, 1, wlen)), st_p[None, None],
            kv5(kvc_s, (1, nb, 1)), kv5(kvs_s, (1, nb, 1)), kv5(win_s, (1, nb, win_s.shape[1])), st_s[None])
```

```python
import functools

import jax
import jax.numpy as jnp
from jax import lax
from jax.experimental import pallas as pl
from jax.experimental.pallas import tpu as pltpu

D_MODEL = 1024
PAGE_SIZE = 128
NSA_HEADS = 8
NSA_KV_HEADS = 2
NSA_GROUP = NSA_HEADS // NSA_KV_HEADS
HEAD_DIM = 64
CMP_LEN = 32
CMP_STRIDE = 16
SEL_BLOCK = 64
N_SEL = 16
WINDOW = 512
RET_HEADS = 4
RET_DK = 128
RET_DV = 256
RET_CHUNK = 128
D_FF = 4 * D_MODEL
ROPE_THETA = 10000.0
RMS_EPS = 1e-6
GN_EPS = 1e-5
NEG_INF = -1e30
FORCE_SCORE = 1e4

NSA_Q_W = NSA_HEADS * HEAD_DIM
NSA_KV_W = 2 * NSA_KV_HEADS * HEAD_DIM
RET_QK_W = RET_HEADS * RET_DK
RET_V_W = RET_HEADS * RET_DV
N_GATES = 3 * NSA_HEADS

LANES = 128
SUBLANES = 8
VMEM_LIMIT_BYTES = 56 * 1024 * 1024

BF16 = jnp.bfloat16
F32 = jnp.float32

QPAD_W = NSA_HEADS * LANES
C_Q = 0
C_KVC = C_Q + QPAD_W
C_KVS = C_KVC + NSA_KV_W
C_KVW = C_KVS + NSA_KV_W
C_RQ = C_KVW + NSA_KV_W
C_RK = C_RQ + RET_QK_W
C_RV = C_RK + RET_QK_W
C_RG = C_RV + RET_V_W
C_GA = C_RG + RET_V_W
C_GB = C_GA + D_MODEL
C_GT = C_GB + D_MODEL
IN_PAD_W = C_GT + LANES


def _cparams(sem):
    return pltpu.CompilerParams(dimension_semantics=sem, vmem_limit_bytes=VMEM_LIMIT_BYTES)


def _vmem_full():
    return pl.BlockSpec(memory_space=pltpu.VMEM)


def _ada_kernel(c_ref, w_ref, b_ref, o_ref):
    c = c_ref[...]
    s = (c * jax.nn.sigmoid(c)).astype(BF16)
    o_ref[...] = jnp.dot(s, w_ref[...], preferred_element_type=F32) + b_ref[...]


def _ada_call(c, w_bf, b):
    rows = c.shape[0]
    n = w_bf.shape[1]
    tn = 1536
    return pl.pallas_call(
        _ada_kernel,
        grid=(n // tn,),
        in_specs=[pl.BlockSpec((rows, D_MODEL), lambda j: (0, 0)),
                  pl.BlockSpec((D_MODEL, tn), lambda j: (0, j)),
                  pl.BlockSpec((1, tn), lambda j: (0, j))],
        out_specs=pl.BlockSpec((rows, tn), lambda j: (0, j)),
        out_shape=jax.ShapeDtypeStruct((rows, n), F32),
        compiler_params=_cparams(("arbitrary",)),
        name="ada_mod",
    )(c, w_bf, b)


def _rope_slab(x, cos, sin_signed, half):
    lane = lax.broadcasted_iota(jnp.int32, x.shape, 1)
    if 2 * half == LANES:
        partner = pltpu.roll(x, half, 1)
    else:
        first = (lane % (2 * half)) < half
        partner = jnp.where(first, pltpu.roll(x, LANES - half, 1), pltpu.roll(x, half, 1))
    return x * cos + partner * sin_signed


def _inproj_kernel(x_ref, g_ref, sc_ref, sh_ref, w_ref, cs64_ref, cs128_ref, *out_refs, transposed):
    if transposed:
        (q_ref, kvc_ref, kvs_ref, kvw_ref, ks_ref, vsT_ref, kw_ref, vwT_ref,
         rq_ref, rk_ref, rv_ref, sg_ref, ga_ref, gb_ref, gt_ref) = out_refs
    else:
        (q_ref, kvc_ref, kvs_ref, kvw_ref,
         rq_ref, rk_ref, rv_ref, sg_ref, ga_ref, gb_ref, gt_ref) = out_refs
    x = x_ref[...]
    ms = jnp.mean(x * x, axis=-1, keepdims=True)
    h = (x * lax.rsqrt(ms + RMS_EPS) * g_ref[...]) * (1.0 + sc_ref[...]) + sh_ref[...]
    hb = h.astype(BF16)
    c64 = cs64_ref[:, :LANES]
    s64 = cs64_ref[:, LANES:]
    c128 = cs128_ref[:, :LANES]
    s128 = cs128_ref[:, LANES:]

    def proj(c0, width):
        return jnp.dot(hb, w_ref[:, c0:c0 + width], preferred_element_type=F32)

    zq = proj(C_Q, QPAD_W)
    for n in range(NSA_HEADS):
        slab = _rope_slab(zq[:, n * LANES:(n + 1) * LANES], c64, s64, HEAD_DIM // 2)
        q_ref[:, n * LANES:(n + 1) * LANES] = (slab * (HEAD_DIM ** -0.5)).astype(BF16)

    kv_refs = (kvc_ref, kvs_ref, kvw_ref)
    for i, c0 in enumerate((C_KVC, C_KVS, C_KVW)):
        z = proj(c0, NSA_KV_W)
        k = _rope_slab(z[:, :LANES], c64, s64, HEAD_DIM // 2)
        v = z[:, LANES:]
        kv_refs[i][:, :LANES] = k
        kv_refs[i][:, LANES:] = v
        if transposed and i > 0:
            k_ref, vT_ref = ((ks_ref, vsT_ref), (kw_ref, vwT_ref))[i - 1]
            k_ref[...] = k.astype(BF16)
            vT = v.T.astype(BF16)
            for t in range(vT_ref.shape[0]):
                vT_ref[t] = vT[:, t * LANES:(t + 1) * LANES]

    zr = proj(C_RQ, RET_QK_W)
    for n in range(RET_HEADS):
        rq_ref[:, n * LANES:(n + 1) * LANES] = _rope_slab(
            zr[:, n * LANES:(n + 1) * LANES], c128, s128, RET_DK // 2).astype(BF16)
    zr = proj(C_RK, RET_QK_W)
    for n in range(RET_HEADS):
        rk_ref[:, n * LANES:(n + 1) * LANES] = (_rope_slab(
            zr[:, n * LANES:(n + 1) * LANES], c128, s128, RET_DK // 2) * (RET_DK ** -0.5)).astype(BF16)
    rv_ref[...] = proj(C_RV, RET_V_W).astype(BF16)
    z = proj(C_RG, RET_V_W)
    sg_ref[...] = (z * jax.nn.sigmoid(z)).astype(BF16)
    ga_ref[...] = jax.nn.sigmoid(proj(C_GA, D_MODEL)).astype(BF16)
    gb_ref[...] = jax.nn.sigmoid(proj(C_GB, D_MODEL)).astype(BF16)
    gt = jax.nn.sigmoid(proj(C_GT, LANES))
    if transposed:
        gt_ref[...] = gt.T
    else:
        gt_ref[...] = gt


def _inproj_call(x, g, sc, sh, w_pad, cs64, cs128, *, tm, transposed):
    m = x.shape[0]
    per_row = sc.shape[0] != 1
    row = lambda i: (i, 0)
    mod_spec = pl.BlockSpec((tm, D_MODEL), row) if per_row else pl.BlockSpec((1, D_MODEL), lambda i: (0, 0))
    nt = tm // LANES
    shp = lambda w, dt: jax.ShapeDtypeStruct((m, w), dt)
    bs = lambda w: pl.BlockSpec((tm, w), row)
    out_shape = [shp(QPAD_W, BF16), shp(NSA_KV_W, F32), shp(NSA_KV_W, F32), shp(NSA_KV_W, F32)]
    out_specs = [bs(QPAD_W), bs(NSA_KV_W), bs(NSA_KV_W), bs(NSA_KV_W)]
    if transposed:
        for _ in range(2):
            out_shape += [shp(LANES, BF16), jax.ShapeDtypeStruct((m // LANES, LANES, LANES), BF16)]
            out_specs += [bs(LANES), pl.BlockSpec((nt, LANES, LANES), lambda i: (i, 0, 0))]
    out_shape += [shp(RET_QK_W, BF16), shp(RET_QK_W, BF16), shp(RET_V_W, BF16), shp(RET_V_W, BF16),
                  shp(D_MODEL, BF16), shp(D_MODEL, BF16)]
    out_specs += [bs(RET_QK_W), bs(RET_QK_W), bs(RET_V_W), bs(RET_V_W), bs(D_MODEL), bs(D_MODEL)]
    if transposed:
        out_shape.append(jax.ShapeDtypeStruct((LANES, m), F32))
        out_specs.append(pl.BlockSpec((LANES, tm), lambda i: (0, i)))
    else:
        out_shape.append(shp(LANES, F32))
        out_specs.append(bs(LANES))
    return pl.pallas_call(
        functools.partial(_inproj_kernel, transposed=transposed),
        grid=(m // tm,),
        in_specs=[pl.BlockSpec((tm, D_MODEL), row),
                  pl.BlockSpec((1, D_MODEL), lambda i: (0, 0)),
                  mod_spec, mod_spec,
                  _vmem_full(),
                  pl.BlockSpec((tm, 2 * LANES), row),
                  pl.BlockSpec((tm, 2 * LANES), row)],
        out_specs=out_specs,
        out_shape=out_shape,
        compiler_params=_cparams(("arbitrary",)),
        name="inproj_t" if transposed else "inproj_n",
    )(x, g, sc, sh, w_pad, cs64, cs128)


def _rope_tables(pos, d):
    half = d // 2
    inv = ROPE_THETA ** (-jnp.arange(0, d, 2, dtype=F32) / d)
    ang = pos.astype(F32)[:, None] * inv[None, :]
    cos = jnp.cos(ang)
    sin = jnp.sin(ang)
    reps = LANES // d
    cos_t = jnp.tile(jnp.concatenate([cos, cos], axis=1), (1, reps))
    sin_t = jnp.tile(jnp.concatenate([-sin, sin], axis=1), (1, reps))
    return jnp.concatenate([cos_t, sin_t], axis=1)


def _pad_in_weights(w_in):
    w = w_in.astype(BF16)
    k = w.shape[0]
    zeros = jnp.zeros((k, HEAD_DIM), BF16)
    cols = []
    for n in range(NSA_HEADS):
        wq = w[:, n * HEAD_DIM:(n + 1) * HEAD_DIM]
        cols += [wq, zeros] if n // NSA_GROUP == 0 else [zeros, wq]
    o = NSA_Q_W
    cols.append(w[:, o:o + 3 * NSA_KV_W])
    o += 3 * NSA_KV_W
    gates = w[:, o:o + N_GATES]
    o += N_GATES
    cols.append(w[:, o:])
    cols.append(gates)
    cols.append(jnp.zeros((k, LANES - N_GATES), BF16))
    return jnp.concatenate(cols, axis=1)


def _compress_kernel(x_ref, xn_ref, w1_ref, w2_ref, b_ref, o_ref, *, n_valid):
    g = x_ref.shape[0]
    x = x_ref[...]
    first = jnp.sum(x * w1_ref[...][None], axis=1)
    second = jnp.sum(x * w2_ref[...][None], axis=1)
    nxt = jnp.sum(xn_ref[0] * w2_ref[...], axis=0, keepdims=True)
    row = lax.broadcasted_iota(jnp.int32, first.shape, 0)
    shifted = jnp.where(row == g - 1, nxt, pltpu.roll(second, g - 1, 0))
    cmp = first + shifted + b_ref[...]
    n = pl.program_id(0) * g + row
    o_ref[...] = jnp.where(n < n_valid, cmp, 0.0)


def _cmp_weights(w_cmp, b_cmp):
    w = jnp.transpose(w_cmp.astype(F32), (2, 0, 1, 3)).reshape(CMP_LEN, NSA_KV_W)
    return w[:CMP_STRIDE], w[CMP_STRIDE:], b_cmp.astype(F32).reshape(1, NSA_KV_W)


def _compress_call(kvc, w1, w2, b):
    t = kvc.shape[0]
    ng = t // CMP_STRIDE
    g = min(ng, 128)
    x3 = kvc.reshape(ng, CMP_STRIDE, NSA_KV_W)
    return pl.pallas_call(
        functools.partial(_compress_kernel, n_valid=ng - 1),
        grid=(ng // g,),
        in_specs=[pl.BlockSpec((g, CMP_STRIDE, NSA_KV_W), lambda i: (i, 0, 0)),
                  pl.BlockSpec((1, CMP_STRIDE, NSA_KV_W), lambda i: (jnp.minimum((i + 1) * g, ng - 1), 0, 0)),
                  pl.BlockSpec((CMP_STRIDE, NSA_KV_W), lambda i: (0, 0)),
                  pl.BlockSpec((CMP_STRIDE, NSA_KV_W), lambda i: (0, 0)),
                  pl.BlockSpec((1, NSA_KV_W), lambda i: (0, 0))],
        out_specs=pl.BlockSpec((g, NSA_KV_W), lambda i: (i, 0)),
        out_shape=jax.ShapeDtypeStruct((ng, NSA_KV_W), F32),
        compiler_params=_cparams(("arbitrary",)),
        name="compress",
    )(x3, x3, w1, w2, b)


def _slab_major(a, nsel):
    r = SEL_BLOCK // CMP_STRIDE
    return a.reshape(nsel, r, a.shape[-1]).transpose(1, 0, 2).reshape(nsel * r, a.shape[-1])


def _nt_dot(a, b):
    return lax.dot_general(a, b, (((1,), (1,)), ((), ())), preferred_element_type=F32)


def _pairs_to_natural(o_ref, accs, h):
    for c in range(NSA_GROUP // 2):
        both = jnp.concatenate([accs[2 * c], accs[2 * c + 1]], axis=0)
        col = (NSA_GROUP // 2) * h + c
        o_ref[:, col * LANES:(col + 1) * LANES] = both.T


def _cmp_attn_kernel(q_ref, ck_ref, cvT_ref, blk_ref, gt_ref, oc_ref, sel_ref, *, tq, nsel, fixed_pos):
    ncp = 4 * nsel
    if fixed_pos is None:
        qpos = pl.program_id(0) * tq + lax.broadcasted_iota(jnp.int32, (1, tq), 1)
    else:
        qpos = jnp.full((1, tq), fixed_pos, jnp.int32)
    valid_c = (CMP_STRIDE * blk_ref[...] + (CMP_LEN - 1)) <= qpos
    jj = lax.broadcasted_iota(jnp.int32, (nsel, tq), 0)
    qblk = qpos // SEL_BLOCK
    valid_s = jj * SEL_BLOCK <= qpos
    forced = (jj == 0) | (jj == qblk) | (jj == qblk - 1)
    ck = ck_ref[0]
    for h in range(NSA_KV_HEADS):
        imp = jnp.zeros((ncp, tq), F32)
        accs = []
        cv = cvT_ref[0, h * HEAD_DIM:(h + 1) * HEAD_DIM, :]
        for g in range(NSA_GROUP):
            n = NSA_GROUP * h + g
            s = _nt_dot(ck, q_ref[:, n * LANES:(n + 1) * LANES])
            s = jnp.where(valid_c, s, NEG_INF)
            e = jnp.exp(s - jnp.max(s, axis=0, keepdims=True))
            p = jnp.where(valid_c, e * (1.0 / jnp.sum(e, axis=0, keepdims=True)), 0.0)
            imp = imp + p
            o = jnp.dot(cv, p.astype(BF16), preferred_element_type=F32)
            accs.append(o * gt_ref[3 * n:3 * n + 1, :])
        _pairs_to_natural(oc_ref, accs, h)
        p3 = imp[3 * nsel:]
        prev = jnp.where(jj == 0, 0.0, pltpu.roll(p3, 1, 0))
        pslc = imp[:nsel] + imp[nsel:2 * nsel] + imp[2 * nsel:3 * nsel] + p3 + prev
        work = jnp.where(valid_s, pslc + jnp.where(forced, FORCE_SCORE, 0.0), NEG_INF)
        sel = jnp.zeros((nsel, tq), jnp.bool_)
        for _ in range(min(N_SEL, nsel)):
            cand = work == jnp.max(work, axis=0, keepdims=True)
            idx = jnp.min(jnp.where(cand, jj, nsel), axis=0, keepdims=True)
            hit = jj == idx
            sel = sel | hit
            work = jnp.where(hit, -jnp.inf, work)
        sel_ref[h] = jnp.where(sel, 0.0, NEG_INF)


def _cmp_attn_call(qpad, ck, cvT, gT, *, tq, fixed_pos=None):
    t = qpad.shape[0]
    nsel = ck.shape[1] // (SEL_BLOCK // CMP_STRIDE)
    per_step = ck.shape[0] != 1
    kv_idx = (lambda i: (i, 0, 0)) if per_step else (lambda i: (0, 0, 0))
    r = jnp.arange(4 * nsel, dtype=jnp.int32)
    blk = jnp.broadcast_to(((SEL_BLOCK // CMP_STRIDE) * (r % nsel) + r // nsel)[:, None], (4 * nsel, tq))
    return pl.pallas_call(
        functools.partial(_cmp_attn_kernel, tq=tq, nsel=nsel, fixed_pos=fixed_pos),
        grid=(t // tq,),
        in_specs=[pl.BlockSpec((tq, QPAD_W), lambda i: (i, 0)),
                  pl.BlockSpec((1, 4 * nsel, LANES), kv_idx),
                  pl.BlockSpec((1, LANES, 4 * nsel), kv_idx),
                  pl.BlockSpec((4 * nsel, tq), lambda i: (0, 0)),
                  pl.BlockSpec((LANES, tq), lambda i: (0, i))],
        out_specs=[pl.BlockSpec((tq, NSA_Q_W), lambda i: (i, 0)),
                   pl.BlockSpec((NSA_KV_HEADS, nsel, tq), lambda i: (0, 0, i))],
        out_shape=[jax.ShapeDtypeStruct((t, NSA_Q_W), F32),
                   jax.ShapeDtypeStruct((NSA_KV_HEADS, nsel, t), F32)],
        compiler_params=_cparams(("arbitrary",)),
        name="cmp_attn_topk_n" if per_step else "cmp_attn_topk_t",
    )(qpad, ck, cvT, blk, gT)


def _stack_q(q_ref, h):
    return jnp.concatenate([q_ref[:, (NSA_GROUP * h + g) * LANES:(NSA_GROUP * h + g + 1) * LANES]
                            for g in range(NSA_GROUP)], axis=0)


def _gate_row(gt_ref, h, branch):
    return jnp.concatenate([gt_ref[3 * (NSA_GROUP * h + g) + branch:3 * (NSA_GROUP * h + g) + branch + 1, :]
                            for g in range(NSA_GROUP)], axis=1)


def _vT_tile(vT_ref, first, count, h):
    v = vT_ref[pl.ds(first, count)]
    return jnp.concatenate([v[t, h * HEAD_DIM:(h + 1) * HEAD_DIM, :] for t in range(count)], axis=1)


def _sel_attn_kernel(q_ref, k_ref, vT_ref, sel_ref, gt_ref, o_ref, *, tq, tk):
    s0 = pl.program_id(0) * tq
    kd = s0 // tk
    nb = tk // SEL_BLOCK
    r = NSA_GROUP * tq
    qpos = s0 + lax.broadcasted_iota(jnp.int32, (1, tq), 1)
    qpos = jnp.concatenate([qpos] * NSA_GROUP, axis=1)
    krow = lax.broadcasted_iota(jnp.int32, (tk, r), 0)
    qhs = [_stack_q(q_ref, h) for h in range(NSA_KV_HEADS)]

    def step(kt, carry, causal):
        kb = k_ref[pl.ds(pl.multiple_of(kt * tk, tk), tk), :]
        out = []
        for h in range(NSA_KV_HEADS):
            m, l, acc = carry[h]
            s = _nt_dot(kb, qhs[h])
            rows = sel_ref[h, pl.ds(pl.multiple_of(kt * nb, nb), nb), :]
            bias = jnp.concatenate([jnp.broadcast_to(rows[b:b + 1], (SEL_BLOCK, tq)) for b in range(nb)], axis=0)
            s = s + jnp.concatenate([bias] * NSA_GROUP, axis=1)
            if causal:
                s = jnp.where(kt * tk + krow <= qpos, s, NEG_INF)
            m_new = jnp.maximum(m, jnp.max(s, axis=0, keepdims=True))
            p = jnp.exp(s - m_new)
            alpha = jnp.exp(m - m_new)
            l = alpha * l + jnp.sum(p, axis=0, keepdims=True)
            pv = jnp.dot(_vT_tile(vT_ref, kt * (tk // LANES), tk // LANES, h), p.astype(BF16),
                         preferred_element_type=F32)
            out.append((m_new, l, alpha * acc + pv))
        return tuple(out)

    init = tuple((jnp.full((1, r), NEG_INF, F32), jnp.zeros((1, r), F32), jnp.zeros((HEAD_DIM, r), F32))
                 for _ in range(NSA_KV_HEADS))
    carry = lax.fori_loop(0, kd, functools.partial(step, causal=False), init)
    final = step(kd, carry, True)
    for h in range(NSA_KV_HEADS):
        m, l, acc = final[h]
        o = acc * (1.0 / l) * _gate_row(gt_ref, h, 1)
        _pairs_to_natural(o_ref, [o[:, g * tq:(g + 1) * tq] for g in range(NSA_GROUP)], h)


def _sel_attn_call(qpad, ks, vsT, selb, gT, *, tq, tk):
    t = qpad.shape[0]
    nsel = t // SEL_BLOCK
    return pl.pallas_call(
        functools.partial(_sel_attn_kernel, tq=tq, tk=tk),
        grid=(t // tq,),
        in_specs=[pl.BlockSpec((tq, QPAD_W), lambda i: (i, 0)),
                  _vmem_full(), _vmem_full(),
                  pl.BlockSpec((NSA_KV_HEADS, nsel, tq), lambda i: (0, 0, i)),
                  pl.BlockSpec((LANES, tq), lambda i: (0, i))],
        out_specs=pl.BlockSpec((tq, NSA_Q_W), lambda i: (i, 0)),
        out_shape=jax.ShapeDtypeStruct((t, NSA_Q_W), F32),
        compiler_params=_cparams(("arbitrary",)),
        name="sel_attn",
    )(qpad, ks, vsT, selb, gT)


def _win_attn_kernel(q_ref, k_ref, vT_ref, gt_ref, o_ref, *, tq):
    band = WINDOW + tq
    s0 = pl.program_id(0) * tq
    start = pl.multiple_of(jnp.maximum(s0 - WINDOW, 0), LANES)
    r = NSA_GROUP * tq
    qpos = s0 + lax.broadcasted_iota(jnp.int32, (1, tq), 1)
    qpos = jnp.concatenate([qpos] * NSA_GROUP, axis=1)
    kp = start + lax.broadcasted_iota(jnp.int32, (band, r), 0)
    valid = (kp <= qpos) & (kp > qpos - WINDOW)
    kb = k_ref[pl.ds(start, band), :]
    for h in range(NSA_KV_HEADS):
        s = jnp.where(valid, _nt_dot(kb, _stack_q(q_ref, h)), NEG_INF)
        e = jnp.exp(s - jnp.max(s, axis=0, keepdims=True))
        p = e * (1.0 / jnp.sum(e, axis=0, keepdims=True))
        o = jnp.dot(_vT_tile(vT_ref, start // LANES, band // LANES, h), p.astype(BF16),
                    preferred_element_type=F32)
        o = o * _gate_row(gt_ref, h, 2)
        _pairs_to_natural(o_ref, [o[:, g * tq:(g + 1) * tq] for g in range(NSA_GROUP)], h)


def _win_attn_call(qpad, kw, vwT, gT, *, tq):
    t = qpad.shape[0]
    return pl.pallas_call(
        functools.partial(_win_attn_kernel, tq=tq),
        grid=(t // tq,),
        in_specs=[pl.BlockSpec((tq, QPAD_W), lambda i: (i, 0)),
                  _vmem_full(), _vmem_full(),
                  pl.BlockSpec((LANES, tq), lambda i: (0, i))],
        out_specs=pl.BlockSpec((tq, NSA_Q_W), lambda i: (i, 0)),
        out_shape=jax.ShapeDtypeStruct((t, NSA_Q_W), F32),
        compiler_params=_cparams(("arbitrary",)),
        name="win_attn",
    )(qpad, kw, vwT, gT)


def _ret_tables(c):
    lg = jnp.log1p(-jnp.exp2(-5.0 - jnp.arange(RET_HEADS, dtype=F32)))
    i = jnp.arange(c, dtype=F32)
    diff = i[:, None] - i[None, :]
    dmat = jnp.where(diff >= 0, jnp.exp(jnp.maximum(diff, 0.0)[None] * lg[:, None, None]), 0.0)
    qdec = jnp.exp((i[None, :] + 1.0) * lg[:, None])
    kdec = jnp.exp((c - 1.0 - i)[None, :] * lg[:, None])
    gc = jnp.exp(c * lg)
    return (dmat,
            jnp.broadcast_to(qdec[:, :, None], (RET_HEADS, c, RET_DV)),
            jnp.broadcast_to(kdec[:, :, None], (RET_HEADS, c, RET_DK)),
            jnp.broadcast_to(gc[:, None, None], (RET_HEADS, SUBLANES, RET_DV)))


def _group_norm_gate(o, sg, gnw, gnb):
    mu = jnp.mean(o, axis=-1, keepdims=True)
    d = o - mu
    var = jnp.mean(d * d, axis=-1, keepdims=True)
    return sg.astype(F32) * (d * lax.rsqrt(var + GN_EPS) * gnw + gnb)


def _ret_kernel(rq_ref, rk_ref, rv_ref, sg_ref, dm_ref, qd_ref, kd_ref, gc_ref, gnw_ref, gnb_ref,
                y_ref, st_ref, s_scr, *, tr):
    @pl.when(pl.program_id(0) == 0)
    def _():
        s_scr[...] = jnp.zeros_like(s_scr)

    c = RET_CHUNK
    for ci in range(tr // c):
        rows = slice(ci * c, (ci + 1) * c)
        for h in range(RET_HEADS):
            q = rq_ref[rows, h * RET_DK:(h + 1) * RET_DK]
            k = rk_ref[rows, h * RET_DK:(h + 1) * RET_DK]
            v = rv_ref[rows, h * RET_DV:(h + 1) * RET_DV]
            att = _nt_dot(q, k) * dm_ref[h]
            s_old = s_scr[h]
            o = jnp.dot(att.astype(BF16), v, preferred_element_type=F32)
            o = o + jnp.dot(q, s_old.astype(BF16), preferred_element_type=F32) * qd_ref[h]
            kdT = (k.astype(F32) * kd_ref[h]).T.astype(BF16)
            s_scr[h] = gc_ref[h][:1] * s_old + jnp.dot(kdT, v, preferred_element_type=F32)
            cols = slice(h * RET_DV, (h + 1) * RET_DV)
            y_ref[rows, cols] = _group_norm_gate(o, sg_ref[rows, cols], gnw_ref[:, cols], gnb_ref[:, cols]).astype(BF16)
    st_ref[...] = s_scr[...]


def _ret_call(rq, rk, rv, sg, gnw, gnb, *, tr):
    t = rq.shape[0]
    dm, qd, kd, gc = _ret_tables(RET_CHUNK)
    row = lambda i: (i, 0)
    full3 = lambda shape: pl.BlockSpec(shape, lambda i: (0, 0, 0))
    return pl.pallas_call(
        functools.partial(_ret_kernel, tr=tr),
        grid=(t // tr,),
        in_specs=[pl.BlockSpec((tr, RET_QK_W), row), pl.BlockSpec((tr, RET_QK_W), row),
                  pl.BlockSpec((tr, RET_V_W), row), pl.BlockSpec((tr, RET_V_W), row),
                  full3(dm.shape), full3(qd.shape), full3(kd.shape), full3(gc.shape),
                  pl.BlockSpec((1, RET_V_W), lambda i: (0, 0)), pl.BlockSpec((1, RET_V_W), lambda i: (0, 0))],
        out_specs=[pl.BlockSpec((tr, RET_V_W), row), full3((RET_HEADS, RET_DK, RET_DV))],
        out_shape=[jax.ShapeDtypeStruct((t, RET_V_W), BF16),
                   jax.ShapeDtypeStruct((RET_HEADS, RET_DK, RET_DV), F32)],
        scratch_shapes=[pltpu.VMEM((RET_HEADS, RET_DK, RET_DV), F32)],
        compiler_params=_cparams(("arbitrary",)),
        name="retention",
    )(rq, rk, rv, sg, dm, qd, kd, gc, gnw, gnb)


def _rms(x, g):
    return x * lax.rsqrt(jnp.mean(x * x, axis=-1, keepdims=True) + RMS_EPS) * g


def _post_kernel(x_ref, oc_ref, os_ref, ow_ref, yr_ref, ga_ref, gb_ref,
                 gta_ref, shf_ref, scf_ref, gtf_ref, gm_ref, gf_ref,
                 wa_ref, wb_ref, wo_ref, wu_ref, wd_ref, y_ref, *, ff_chunk):
    o_nsa = (oc_ref[...] + os_ref[...] + ow_ref[...]).astype(BF16)
    y_a = jnp.dot(o_nsa, wa_ref[...], preferred_element_type=F32)
    y_b = jnp.dot(yr_ref[...], wb_ref[...], preferred_element_type=F32)
    merged = ga_ref[...].astype(F32) * y_a + gb_ref[...].astype(F32) * y_b
    mixed = jnp.dot(merged.astype(BF16), wo_ref[...], preferred_element_type=F32)
    x1 = x_ref[...] + gta_ref[...] * mixed
    h2 = (_rms(x1, gm_ref[...]) * (1.0 + scf_ref[...]) + shf_ref[...]).astype(BF16)
    mlp = jnp.zeros_like(x1)
    for c0 in range(0, D_FF, ff_chunk):
        u = jnp.maximum(jnp.dot(h2, wu_ref[:, c0:c0 + ff_chunk], preferred_element_type=F32), 0.0)
        mlp = mlp + jnp.dot((u * u).astype(BF16), wd_ref[c0:c0 + ff_chunk, :], preferred_element_type=F32)
    x2 = x1 + gtf_ref[...] * mlp
    y_ref[...] = _rms(x2, gf_ref[...])


def _post_call(x, oc, osel, ow, yr, ga, gb, gta, shf, scf, gtf, gm, gf, wa, wb, wo, wu, wd, *, tm):
    m = x.shape[0]
    per_row = gta.shape[0] != 1
    row = lambda i: (i, 0)
    one = lambda i: (0, 0)
    mod_spec = pl.BlockSpec((tm, D_MODEL), row) if per_row else pl.BlockSpec((1, D_MODEL), one)
    return pl.pallas_call(
        functools.partial(_post_kernel, ff_chunk=1024),
        grid=(m // tm,),
        in_specs=[pl.BlockSpec((tm, D_MODEL), row),
                  pl.BlockSpec((tm, NSA_Q_W), row), pl.BlockSpec((tm, NSA_Q_W), row), pl.BlockSpec((tm, NSA_Q_W), row),
                  pl.BlockSpec((tm, RET_V_W), row), pl.BlockSpec((tm, D_MODEL), row), pl.BlockSpec((tm, D_MODEL), row),
                  mod_spec, mod_spec, mod_spec, mod_spec,
                  pl.BlockSpec((1, D_MODEL), one), pl.BlockSpec((1, D_MODEL), one),
                  _vmem_full(), _vmem_full(), _vmem_full(), _vmem_full(), _vmem_full()],
        out_specs=pl.BlockSpec((tm, D_MODEL), row),
        out_shape=jax.ShapeDtypeStruct((m, D_MODEL), F32),
        compiler_params=_cparams(("arbitrary",)),
        name="post_t" if not per_row else "post_n",
    )(x, oc, osel, ow, yr, ga, gb, gta, shf, scf, gtf, gm, gf, wa, wb, wo, wu, wd)


def _prompt_path(x, mod, wts, pos, w_cmp, b_cmp):
    t = x.shape[0]
    sh_a, sc_a, gt_a, sh_f, sc_f, gt_f = [mod[:, i * D_MODEL:(i + 1) * D_MODEL] for i in range(6)]
    (q, kvc, kvs, kvw, ks, vsT, kw, vwT, rq, rk, rv, sg, ga, gb, gT) = _inproj_call(
        x, wts["norm_mix"], sc_a, sh_a, wts["w_in"], _rope_tables(pos, HEAD_DIM), _rope_tables(pos, RET_DK),
        tm=512, transposed=True)
    w1, w2, b = _cmp_weights(w_cmp, b_cmp)
    cmp = _compress_call(kvc, w1, w2, b)
    nsel = t // SEL_BLOCK
    ck = _slab_major(cmp[:, :LANES], nsel).astype(BF16)
    cvT = _slab_major(cmp[:, LANES:], nsel).T.astype(BF16)
    oc, selb = _cmp_attn_call(q, ck[None], cvT[None], gT, tq=128)
    osel = _sel_attn_call(q, ks, vsT, selb, gT, tq=128, tk=1024)
    ow = _win_attn_call(q, kw, vwT, gT, tq=128)
    yr, state = _ret_call(rq, rk, rv, sg, wts["gn_w"], wts["gn_b"], tr=512)
    y = _post_call(x, oc, osel, ow, yr, ga, gb, gt_a, sh_f, sc_f, gt_f, wts["norm_mlp"], wts["norm_final"],
                   wts["w_a"], wts["w_b"], wts["w_o"], wts["w_u"], wts["w_d"], tm=256)
    return y, kvc, kvs, kvw, state


PAGES_PER_STEP = 16
GROUPS_PER_PAGE = PAGE_SIZE // CMP_STRIDE


def _paged_compress_kernel(pt_ref, *refs):
    page_refs = refs[:PAGES_PER_STEP]
    nxt_ref, new_ref, w1_ref, w2_ref, b_ref, o_ref = refs[PAGES_PER_STEP:]
    x = jnp.concatenate([r[0] for r in page_refs], axis=0)
    g = x.shape[0]
    first = jnp.sum(x * w1_ref[...][None], axis=1)
    second = jnp.sum(x * w2_ref[...][None], axis=1)
    nxt_page = jnp.sum(nxt_ref[0, 0] * w2_ref[...], axis=0, keepdims=True)
    nxt_new = new_ref[0] * w2_ref[0:1, :]
    last = pl.program_id(1) == pl.num_programs(1) - 1
    nxt = jnp.where(last, nxt_new, nxt_page)
    row = lax.broadcasted_iota(jnp.int32, first.shape, 0)
    shifted = jnp.where(row == g - 1, nxt, pltpu.roll(second, g - 1, 0))
    o_ref[0] = first + shifted + b_ref[...]


def _paged_compress_call(cache, page_table, new_row, w1, w2, b):
    nb, npages = page_table.shape
    steps = npages // PAGES_PER_STEP
    c4 = cache.reshape(cache.shape[0], GROUPS_PER_PAGE, CMP_STRIDE, NSA_KV_W)
    page_spec = lambda k: pl.BlockSpec((1, GROUPS_PER_PAGE, CMP_STRIDE, NSA_KV_W),
                                       lambda bi, i, pt: (pt[bi, PAGES_PER_STEP * i + k], 0, 0, 0))
    nxt_spec = pl.BlockSpec((1, GROUPS_PER_PAGE, CMP_STRIDE, NSA_KV_W),
                            lambda bi, i, pt: (pt[bi, jnp.minimum(PAGES_PER_STEP * (i + 1), npages - 1)], 0, 0, 0))
    const = lambda shape: pl.BlockSpec(shape, lambda bi, i, pt: (0, 0))
    g = PAGES_PER_STEP * GROUPS_PER_PAGE
    return pl.pallas_call(
        _paged_compress_kernel,
        grid_spec=pltpu.PrefetchScalarGridSpec(
            num_scalar_prefetch=1,
            grid=(nb, steps),
            in_specs=[page_spec(k) for k in range(PAGES_PER_STEP)] + [
                nxt_spec,
                pl.BlockSpec((1, 1, NSA_KV_W), lambda bi, i, pt: (bi, 0, 0)),
                const((CMP_STRIDE, NSA_KV_W)), const((CMP_STRIDE, NSA_KV_W)), const((1, NSA_KV_W))],
            out_specs=pl.BlockSpec((1, g, NSA_KV_W), lambda bi, i, pt: (bi, i, 0)),
        ),
        out_shape=jax.ShapeDtypeStruct((nb, npages * GROUPS_PER_PAGE, NSA_KV_W), F32),
        compiler_params=_cparams(("arbitrary", "arbitrary")),
        name="paged_compress",
    )(page_table, *([c4] * (PAGES_PER_STEP + 1)), new_row, w1, w2, b)


def _softmax_lanes(s):
    e = jnp.exp(s - jnp.max(s, axis=-1, keepdims=True))
    return e * (1.0 / jnp.sum(e, axis=-1, keepdims=True))


def _gather_blocks_kernel(idx_ref, pt_ref, *refs):
    o_ref = refs[N_SEL]
    for k in range(N_SEL):
        o_ref[0, 0, k] = refs[k][0]


def _gather_blocks_call(idx, page_table, cache5):
    nb = idx.shape[0]
    bpp = PAGE_SIZE // SEL_BLOCK
    n_past_blk = page_table.shape[1] * bpp
    tail = cache5.shape[2:]

    def blk_map(k):
        def f(bi, h, idx_ref, pt_ref):
            j = jnp.minimum(idx_ref[bi, h * N_SEL + k], n_past_blk - 1)
            return (pt_ref[bi, j // bpp], j % bpp, 0, 0, 0)
        return f

    return pl.pallas_call(
        _gather_blocks_kernel,
        grid_spec=pltpu.PrefetchScalarGridSpec(
            num_scalar_prefetch=2,
            grid=(nb, NSA_KV_HEADS),
            in_specs=[pl.BlockSpec((1, SEL_BLOCK) + tail, blk_map(k)) for k in range(N_SEL)],
            out_specs=pl.BlockSpec((1, 1, N_SEL, SEL_BLOCK) + tail,
                                   lambda bi, h, idx_ref, pt_ref: (bi, h, 0, 0, 0, 0, 0)),
        ),
        out_shape=jax.ShapeDtypeStruct((nb, NSA_KV_HEADS, N_SEL, SEL_BLOCK) + tail, F32),
        compiler_params=_cparams(("arbitrary", "arbitrary")),
        name="sample_gather_blocks",
    )(idx.reshape(nb, NSA_KV_HEADS * N_SEL), page_table, *([cache5] * N_SEL))


def _sample_sel_kernel(idx_ref, q_ref, kv_ref, new_ref, g_ref, o_ref, *, n_past_blk):
    bi = pl.program_id(0)
    h = pl.program_id(1)
    kv = kv_ref[0, 0]
    nk = kv.shape[0]
    row = lax.broadcasted_iota(jnp.int32, (nk, LANES), 0)
    lane = lax.broadcasted_iota(jnp.int32, (SUBLANES, nk), 1)
    first_new = jnp.zeros((nk, LANES), jnp.int32)
    dead = jnp.zeros((SUBLANES, nk), jnp.int32)
    for kk in range(N_SEL):
        is_new = (idx_ref[bi, h * N_SEL + kk] >= n_past_blk).astype(jnp.int32)
        first_new = first_new + jnp.where(row == kk * SEL_BLOCK, is_new, 0)
        dead = dead + jnp.where((lane > kk * SEL_BLOCK) & (lane < (kk + 1) * SEL_BLOCK), is_new, 0)
    new = new_ref[0]
    k_all = jnp.where(first_new > 0, new[:, :LANES], kv[:, :LANES]).astype(BF16)
    v_all = jnp.where(first_new > 0, new[:, LANES:], kv[:, LANES:]).astype(BF16)
    s = _nt_dot(q_ref[0, 0], k_all)
    p = _softmax_lanes(jnp.where(dead > 0, NEG_INF, s))
    o_ref[0, 0] = jnp.dot(p.astype(BF16), v_all, preferred_element_type=F32) * g_ref[0, 0]


def _sample_sel_call(idx, n_past_blk, q4, kv_sel, new_row, g4):
    nb = q4.shape[0]
    bh = lambda bi, h, idx_ref: (bi, h, 0, 0)
    return pl.pallas_call(
        functools.partial(_sample_sel_kernel, n_past_blk=n_past_blk),
        grid_spec=pltpu.PrefetchScalarGridSpec(
            num_scalar_prefetch=1,
            grid=(nb, NSA_KV_HEADS),
            in_specs=[pl.BlockSpec((1, 1, SUBLANES, LANES), bh),
                      pl.BlockSpec((1, 1) + kv_sel.shape[2:], bh),
                      pl.BlockSpec((1, 1, NSA_KV_W), lambda bi, h, idx_ref: (bi, 0, 0)),
                      pl.BlockSpec((1, 1, SUBLANES, LANES), bh)],
            out_specs=pl.BlockSpec((1, 1, SUBLANES, LANES), bh),
        ),
        out_shape=jax.ShapeDtypeStruct((nb, NSA_KV_HEADS, SUBLANES, LANES), F32),
        compiler_params=_cparams(("arbitrary", "arbitrary")),
        name="sample_sel_attn",
    )(idx.reshape(nb, NSA_KV_HEADS * N_SEL), q4, kv_sel, new_row, g4)


def _sample_win_kernel(q_ref, cw_ref, new_ref, g_ref, nw_ref, o_ref):
    cw = cw_ref[0]
    w = cw.shape[0]
    row = lax.broadcasted_iota(jnp.int32, cw.shape, 0)
    nw = jnp.where(row == w - 1, new_ref[0], pltpu.roll(cw, w - 1, 0))
    nw_ref[0] = nw
    k = nw[:, :LANES].astype(BF16)
    v = nw[:, LANES:].astype(BF16)
    for h in range(NSA_KV_HEADS):
        p = _softmax_lanes(_nt_dot(q_ref[0, h], k))
        o_ref[0, h] = jnp.dot(p.astype(BF16), v, preferred_element_type=F32) * g_ref[0, h]


def _sample_win_call(q4, cache_win, new_row, g4):
    nb, w = cache_win.shape[:2]
    b4 = lambda bi: (bi, 0, 0, 0)
    b3 = lambda bi: (bi, 0, 0)
    return pl.pallas_call(
        _sample_win_kernel,
        grid=(nb,),
        in_specs=[pl.BlockSpec((1, NSA_KV_HEADS, SUBLANES, LANES), b4),
                  pl.BlockSpec((1, w, NSA_KV_W), b3),
                  pl.BlockSpec((1, 1, NSA_KV_W), b3),
                  pl.BlockSpec((1, NSA_KV_HEADS, SUBLANES, LANES), b4)],
        out_specs=[pl.BlockSpec((1, w, NSA_KV_W), b3),
                   pl.BlockSpec((1, NSA_KV_HEADS, SUBLANES, LANES), b4)],
        out_shape=[jax.ShapeDtypeStruct((nb, w, NSA_KV_W), F32),
                   jax.ShapeDtypeStruct((nb, NSA_KV_HEADS, SUBLANES, LANES), F32)],
        compiler_params=_cparams(("arbitrary",)),
        name="sample_win_attn",
    )(q4, cache_win, new_row, g4)


def _sample_ret_kernel(q_ref, k_ref, kc_ref, v_ref, sg_ref, s_ref, dec_ref, gnw_ref, gnb_ref, y_ref, so_ref):
    for h in range(RET_HEADS):
        q = q_ref[0, h]
        k = k_ref[0, h].astype(F32)
        v = v_ref[0, h].astype(F32)
        s_old = s_ref[0, h]
        gamma = dec_ref[h]
        qk = jnp.sum(q.astype(F32) * k, axis=-1, keepdims=True)
        o = qk * v + jnp.dot(q, s_old.astype(BF16), preferred_element_type=F32) * gamma
        so_ref[0, h] = gamma[:1] * s_old + kc_ref[0, h] * v[:1]
        cols = slice(h * RET_DV, (h + 1) * RET_DV)
        y_ref[0, h] = _group_norm_gate(o, sg_ref[0, h], gnw_ref[:, cols], gnb_ref[:, cols])


def _sample_ret_call(q4, k4, kc4, v4, sg4, state, gnw, gnb):
    nb = q4.shape[0]
    _, _, _, gc = _ret_tables(1)
    b4 = lambda bi: (bi, 0, 0, 0)
    spec = lambda a: pl.BlockSpec((1,) + a.shape[1:], b4)
    return pl.pallas_call(
        _sample_ret_kernel,
        grid=(nb,),
        in_specs=[spec(q4), spec(k4), spec(kc4), spec(v4), spec(sg4), spec(state),
                  pl.BlockSpec(gc.shape, lambda bi: (0, 0, 0)),
                  pl.BlockSpec((1, RET_V_W), lambda bi: (0, 0)), pl.BlockSpec((1, RET_V_W), lambda bi: (0, 0))],
        out_specs=[pl.BlockSpec((1, RET_HEADS, SUBLANES, RET_DV), b4), spec(state)],
        out_shape=[jax.ShapeDtypeStruct((nb, RET_HEADS, SUBLANES, RET_DV), F32),
                   jax.ShapeDtypeStruct(state.shape, F32)],
        compiler_params=_cparams(("arbitrary",)),
        name="sample_retention",
    )(q4, k4, kc4, v4, sg4, state, gc, gnw, gnb)


def _row0(a, rows):
    return jnp.concatenate([a[..., None, :], jnp.zeros(a.shape[:-1] + (rows - 1, a.shape[-1]), a.dtype)], axis=-2)


def _sample_path(x, mod, wts, pos, w_cmp, b_cmp, cache_cmp, cache_sel, cache_win, state, page_table):
    nb = x.shape[0]
    p_len = page_table.shape[1] * PAGE_SIZE
    sh_a, sc_a, gt_a, sh_f, sc_f, gt_f = [mod[:, i * D_MODEL:(i + 1) * D_MODEL] for i in range(6)]
    (q, kvc, kvs, kvw, rq, rk, rv, sg, ga, gb, gates) = _inproj_call(
        x, wts["norm_mix"], sc_a, sh_a, wts["w_in"], _rope_tables(pos, HEAD_DIM), _rope_tables(pos, RET_DK),
        tm=nb, transposed=False)
    w1, w2, b = _cmp_weights(w_cmp, b_cmp)
    n_phys = cache_cmp.shape[0]
    cmp_main = _paged_compress_call(cache_cmp.reshape(n_phys, PAGE_SIZE, NSA_KV_W), page_table, kvc[:, None, :], w1, w2, b)
    nc_main = cmp_main.shape[1]
    tail = jnp.concatenate([(kvc * w1[0:1] + b)[:, None, :],
                            jnp.broadcast_to(b[None], (nb, 2, NSA_KV_W))], axis=1)
    nsel = -(-(p_len // SEL_BLOCK + 1) // 32) * 32
    ncp = nsel * (SEL_BLOCK // CMP_STRIDE)
    cmp_all = jnp.concatenate([cmp_main, tail, jnp.zeros((nb, ncp - nc_main - 3, NSA_KV_W), F32)], axis=1)
    slab = jax.vmap(lambda a: _slab_major(a, nsel))
    ck = slab(cmp_all[:, :, :LANES]).astype(BF16)
    cvT = jnp.swapaxes(slab(cmp_all[:, :, LANES:]), 1, 2).astype(BF16)
    tq = LANES
    q_lane0 = _row0(q, tq).reshape(nb * tq, QPAD_W)
    gT_lane0 = jnp.transpose(_row0(gates, tq), (2, 0, 1)).reshape(LANES, nb * tq)
    oc_rows, selb = _cmp_attn_call(q_lane0, ck, cvT, gT_lane0, tq=tq, fixed_pos=p_len)
    oc = oc_rows.reshape(nb, tq, NSA_Q_W)[:, 0]
    chosen = selb.reshape(NSA_KV_HEADS, nsel, nb, tq)[:, :, :, 0] == 0.0
    order = jnp.argsort(jnp.where(chosen, 0, 1), axis=1, stable=True)[:, :N_SEL]
    idx = jnp.transpose(order, (2, 0, 1)).astype(jnp.int32)
    q4 = q.reshape(nb, NSA_KV_HEADS, NSA_GROUP, LANES)
    q4 = jnp.concatenate([q4, jnp.zeros_like(q4)], axis=2)
    g3 = gates[:, :N_GATES].reshape(nb, NSA_KV_HEADS, NSA_GROUP, 3)
    g4 = lambda br: jnp.broadcast_to(
        jnp.concatenate([g3[..., br], jnp.zeros_like(g3[..., br])], axis=2)[..., None], (nb, NSA_KV_HEADS, SUBLANES, LANES))
    kv_sel = _gather_blocks_call(idx, page_table, cache_sel).reshape(nb, NSA_KV_HEADS, N_SEL * SEL_BLOCK, NSA_KV_W)
    os_raw = _sample_sel_call(idx, p_len // SEL_BLOCK, q4, kv_sel, kvs[:, None, :], g4(1))
    new_win, ow_raw = _sample_win_call(q4, cache_win.reshape(nb, -1, NSA_KV_W), kvw[:, None, :], g4(2))

    def natural(raw):
        parts = [raw[:, h, :NSA_GROUP, h * HEAD_DIM:(h + 1) * HEAD_DIM] for h in range(NSA_KV_HEADS)]
        return jnp.concatenate(parts, axis=1).reshape(nb, NSA_Q_W)

    rq4 = _row0(rq.reshape(nb, RET_HEADS, RET_DK), SUBLANES)
    rk3 = rk.reshape(nb, RET_HEADS, RET_DK)
    y_raw, new_state = _sample_ret_call(
        rq4, _row0(rk3, SUBLANES), rk3.astype(F32)[..., None], _row0(rv.reshape(nb, RET_HEADS, RET_DV), SUBLANES),
        _row0(sg.reshape(nb, RET_HEADS, RET_DV), SUBLANES), state, wts["gn_w"], wts["gn_b"])
    yr = y_raw[:, :, 0, :].reshape(nb, RET_V_W).astype(BF16)
    y = _post_call(x, oc, natural(os_raw), natural(ow_raw), yr, ga, gb, gt_a, sh_f, sc_f, gt_f,
                   wts["norm_mlp"], wts["norm_final"], wts["w_a"], wts["w_b"], wts["w_o"], wts["w_u"], wts["w_d"], tm=nb)
    return y, kvc, kvs, new_win, new_state


def kernel(x_prompt, x_sample, c_prompt, c_sample, cache_cmp_kv, cache_sel_kv, cache_win_kv, state_ret, page_table,
           norm_mix, norm_mlp, norm_final, w_ada, b_ada, w_in, w_cmp, b_cmp, ret_gn_w, ret_gn_b,
           w_branch_nsa, w_branch_ret, w_out, w_up, w_down):
    assert x_prompt.shape[0] == 1 and x_sample.shape[1] == 1 and norm_mix.shape[0] == 1
    t = x_prompt.shape[1]
    nb = x_sample.shape[0]
    p_len = page_table.shape[1] * PAGE_SIZE
    wts = dict(norm_mix=norm_mix[0][None], norm_mlp=norm_mlp[0][None], norm_final=norm_final[None],
               w_in=_pad_in_weights(w_in[0]), gn_w=ret_gn_w[0][None], gn_b=ret_gn_b[0][None],
               w_a=w_branch_nsa[0].astype(BF16), w_b=w_branch_ret[0].astype(BF16), w_o=w_out[0].astype(BF16),
               w_u=w_up[0].astype(BF16), w_d=w_down[0].astype(BF16))
    c_all = jnp.concatenate([c_prompt, c_sample], axis=0)
    c_all = jnp.pad(c_all, ((0, (-c_all.shape[0]) % SUBLANES), (0, 0)))
    mod = _ada_call(c_all, w_ada[0].astype(BF16), b_ada[0][None])
    yp, kvc_p, kvs_p, kvw_p, st_p = _prompt_path(x_prompt[0], mod[:1], wts, jnp.arange(t), w_cmp[0], b_cmp[0])
    ys, kvc_s, kvs_s, win_s, st_s = _sample_path(
        x_sample[:, 0], mod[1:1 + nb], wts, jnp.full((nb,), p_len), w_cmp[0], b_cmp[0],
        cache_cmp_kv[0], cache_sel_kv[0], cache_win_kv[0], state_ret[0], page_table)
    kv5 = lambda a, lead: a.reshape(lead + (2, NSA_KV_HEADS, HEAD_DIM))
    wlen = min(WINDOW, t)
    return (yp[None], ys[:, None, :],
            kv5(kvc_p, (1, 1, t)), kv5(kvs_p, (1, 1, t)), kv5(kvw_p[t - wlen:], (1, 1, wlen)), st_p[None, None],
            kv5(kvc_s, (1, nb, 1)), kv5(kvs_s, (1, nb, 1)), kv5(win_s, (1, nb, win_s.shape[1])), st_s[None])
```
